```python
import math
import jax
import jax.numpy as jnp
from jax import lax
import numpy as np

D_MODEL = 4096
BATCH = 4
SEQ = 2048
DEPTH = 2
DEC_BATCH = 8
DEC_SEQ = 1
PAST_LEN = 16384
PAGE_SIZE = 128

HEAD_DIM = 128
FOX_HEADS = 12
NSA_HEADS = 12
NSA_KV_HEADS = 3
NSA_GROUP = NSA_HEADS // NSA_KV_HEADS
DIFF_HEADS = 4
DIFF_VDIM = 2 * HEAD_DIM
CMP_BLOCK = 64
N_SELECT = 16
WINDOW = 512
N_BUCKETS = 32
MAX_EXACT = N_BUCKETS // 2
MAX_DISTANCE = 1024
D_FF = 11008
Q_BLOCK = 128
SLC_Q_BLOCK = 32
FORCE_SCORE = 1.0e4
FORGET_BIAS_INIT = 4.0
NEG_INF = -1.0e30
EPS = 1.0e-6
IN_SIZES = (FOX_HEADS * HEAD_DIM, FOX_HEADS * HEAD_DIM, FOX_HEADS * HEAD_DIM, FOX_HEADS,
            NSA_HEADS * HEAD_DIM, 6 * NSA_KV_HEADS * HEAD_DIM, 3 * NSA_HEADS,
            2 * DIFF_HEADS * HEAD_DIM, 2 * DIFF_HEADS * HEAD_DIM, DIFF_HEADS * DIFF_VDIM,
            3 * D_MODEL)
D_IN = sum(IN_SIZES)

kernel_name = 'hybrid_fox_nsa_diff_decode_step'


def _split_points():
    pts, acc = [], 0
    for s in IN_SIZES[:-1]:
        acc += s
        pts.append(acc)
    return pts


def rms_norm(x, g):
    xf = x.astype(jnp.float32)
    y = xf * lax.rsqrt(jnp.mean(xf * xf, axis=-1, keepdims=True) + EPS)
    return (y * g.astype(jnp.float32)).astype(x.dtype)


def swiglu(x, wg, wu, wd):
    return (jax.nn.silu(x @ wg) * (x @ wu)) @ wd


def masked_softmax(logits, mask):
    p = jax.nn.softmax(jnp.where(mask, logits, NEG_INF), axis=-1)
    return jnp.where(mask, p, 0.0)


def t5_bucket(dist):
    n = jnp.maximum(dist, 0)
    nf = jnp.maximum(n, 1).astype(jnp.float32)
    large = MAX_EXACT + (jnp.log(nf / MAX_EXACT) / math.log(MAX_DISTANCE / MAX_EXACT)
                         * (N_BUCKETS - MAX_EXACT)).astype(jnp.int32)
    return jnp.where(n < MAX_EXACT, n, jnp.minimum(large, N_BUCKETS - 1))


def sweep(fn, block, qpos, *arrs):
    T = qpos.shape[0]
    if T <= block or T % block:
        return fn(qpos, *arrs)
    n = T // block
    split = lambda a: jnp.moveaxis(a.reshape((a.shape[0], n, block) + a.shape[2:]), 1, 0)
    out = lax.map(lambda xs: fn(xs[0], *xs[1]), (qpos.reshape(n, block), tuple(split(a) for a in arrs)))
    out = jnp.moveaxis(out, 0, 1)
    return out.reshape((out.shape[0], T) + out.shape[3:])


def fox_attend(q, qpos, k, v, logf):
    L = k.shape[1]
    F = jnp.cumsum(logf.astype(jnp.float32), axis=1)
    Fq = jnp.take(F, qpos, axis=1)
    Fk = jnp.moveaxis(F, 1, 2)[:, :, None, :]
    kpos = jnp.arange(L, dtype=jnp.int32)
    scale = HEAD_DIM ** -0.5

    def blk(qp, qb, fq):
        logits = jnp.einsum('bthd,bshd->bhts', qb, k).astype(jnp.float32) * scale
        logits = logits + jnp.moveaxis(fq, 1, 2)[..., None] - Fk
        p = masked_softmax(logits, kpos[None, :] <= qp[:, None])
        return jnp.einsum('bhts,bshd->bthd', p.astype(v.dtype), v)

    return sweep(blk, Q_BLOCK, qpos, q, Fq)


def diff_attend(q, qpos, k, v, lam, tbl):
    L = k.shape[1]
    kpos = jnp.arange(L, dtype=jnp.int32)
    scale = HEAD_DIM ** -0.5

    def blk(qp, qb):
        dist = qp[:, None] - kpos[None, :]
        bias = jnp.moveaxis(tbl[t5_bucket(dist)], -1, 0).astype(jnp.float32)
        logits = jnp.einsum('bthcd,bshcd->bhcts', qb, k).astype(jnp.float32) * scale + bias[None, :, None]
        a = masked_softmax(logits, dist >= 0)
        w = a[:, :, 0] - lam * a[:, :, 1]
        return jnp.einsum('bhts,bshe->bthe', w.astype(v.dtype), v)

    return sweep(blk, Q_BLOCK, qpos, q)


def compress(blocks, w1, w2, pos):
    h = jax.nn.gelu(jnp.einsum('bnlkd,lde->bnke', blocks + pos[:, None, :], w1))
    return jnp.einsum('bnke,ef->bnkf', h, w2)


def nsa_attend(q, qpos, kc, vc, ks, vs, kw, vw, gate, phi_w1, phi_w2, phi_pos, tbl):
    B, L = kc.shape[:2]
    nb = -(-L // CMP_BLOCK)
    Lp = nb * CMP_BLOCK
    scale = HEAD_DIM ** -0.5
    blocks = lambda a: jnp.pad(a, ((0, 0), (0, Lp - L), (0, 0), (0, 0))).reshape(B, nb, CMP_BLOCK, NSA_KV_HEADS, HEAD_DIM)
    k_cmp = compress(blocks(kc), phi_w1[0], phi_w2[0], phi_pos[0])
    v_cmp = compress(blocks(vc), phi_w1[1], phi_w2[1], phi_pos[1])
    ks_b = jnp.moveaxis(blocks(ks), 3, 1)
    vs_b = jnp.moveaxis(blocks(vs), 3, 1)
    blk_id = jnp.arange(nb, dtype=jnp.int32)
    blk_end = blk_id * CMP_BLOCK + CMP_BLOCK - 1
    n_sel = min(N_SELECT, nb)
    tbl_kg = jnp.moveaxis(tbl.reshape(N_BUCKETS, NSA_KV_HEADS, NSA_GROUP), 1, 0)
    b_idx = jnp.arange(B)[:, None, None, None]
    g_idx = jnp.arange(NSA_KV_HEADS)[None, :, None, None]
    in_blk = jnp.arange(CMP_BLOCK, dtype=jnp.int32)

    def cmp_slc(qp, qb):
        t = qp.shape[0]
        lc = jnp.einsum('btkgd,bnkd->bkgtn', qb, k_cmp).astype(jnp.float32) * scale
        pc = masked_softmax(lc, blk_end[None, :] <= qp[:, None])
        o_cmp = jnp.einsum('bkgtn,bnkd->btkgd', pc.astype(v_cmp.dtype), v_cmp)
        cur = qp[:, None] // CMP_BLOCK
        forced = (blk_id == 0) | (blk_id == cur) | (blk_id == cur - 1)
        score = jnp.where(forced, FORCE_SCORE, pc.sum(axis=2))
        score = jnp.where(blk_id * CMP_BLOCK <= qp[:, None], score, -1.0)
        _, idx = lax.top_k(score, n_sel)
        k_sel = ks_b[b_idx, g_idx, idx].reshape(B, NSA_KV_HEADS, t, n_sel * CMP_BLOCK, HEAD_DIM)
        v_sel = vs_b[b_idx, g_idx, idx].reshape(B, NSA_KV_HEADS, t, n_sel * CMP_BLOCK, HEAD_DIM)
        spos = (idx[..., None] * CMP_BLOCK + in_blk).reshape(B, NSA_KV_HEADS, t, n_sel * CMP_BLOCK)
        dist = qp[None, None, :, None] - spos
        bias = jnp.moveaxis(tbl_kg[g_idx, t5_bucket(dist)], -1, 2).astype(jnp.float32)
        ls = jnp.einsum('btkgd,bktsd->bkgts', qb, k_sel).astype(jnp.float32) * scale + bias
        ps = masked_softmax(ls, (dist >= 0)[:, :, None])
        o_slc = jnp.einsum('bkgts,bktsd->btkgd', ps.astype(v_sel.dtype), v_sel)
        return jnp.stack([o_cmp, o_slc], axis=2)

    o_cs = sweep(cmp_slc, SLC_Q_BLOCK, qpos, q)
    q0 = qpos[0]

    def win(qp, qb):
        t = qp.shape[0]
        start = qp[0] - q0
        kb = lax.dynamic_slice_in_dim(kw, start, WINDOW + t, axis=1)
        vb = lax.dynamic_slice_in_dim(vw, start, WINDOW + t, axis=1)
        kpos = qp[0] - WINDOW + jnp.arange(WINDOW + t, dtype=jnp.int32)
        dist = qp[:, None] - kpos[None, :]
        mask = (dist >= 0) & (dist < WINDOW) & (kpos >= 0)[None, :]
        bias = jnp.transpose(tbl[t5_bucket(dist)], (2, 0, 1)).reshape(NSA_KV_HEADS, NSA_GROUP, t, WINDOW + t)
        lw = jnp.einsum('btkgd,bskd->bkgts', qb, kb).astype(jnp.float32) * scale + bias.astype(jnp.float32)
        pw = masked_softmax(lw, mask)
        return jnp.einsum('bkgts,bskd->btkgd', pw.astype(vb.dtype), vb)

    o_win = sweep(win, Q_BLOCK, qpos, q)
    return gate[..., 0:1] * o_cs[:, :, 0] + gate[..., 1:2] * o_cs[:, :, 1] + gate[..., 2:3] * o_win


def gather_pages(pool, page_table, l):
    g = pool[page_table, l]
    return g.reshape((g.shape[0], g.shape[1] * g.shape[2]) + g.shape[3:])


def trunk_layer(x, qpos, past, l, norm_g, w_in, fox_fbias, nsa_phi_w1, nsa_phi_w2, nsa_phi_pos,
                diff_lambda, diff_subln_g, w_br_fox, w_br_nsa, w_br_diff, w_o,
                ffn_w_gate, ffn_w_up, ffn_w_down, rel_bias):
    B, T, _ = x.shape
    g = norm_g[l]
    ffn = lambda z, i: swiglu(z, ffn_w_gate[l, i], ffn_w_up[l, i], ffn_w_down[l, i])
    h = x + 0.5 * rms_norm(ffn(rms_norm(x, g[0]), 0), g[1])
    u = rms_norm(h, g[2])
    fq, fk, fv, ff, nq, nkv, ng, dq, dk, dv, bg = jnp.split(u @ w_in[l], _split_points(), axis=-1)
    fq = fq.reshape(B, T, FOX_HEADS, HEAD_DIM)
    fk = fk.reshape(B, T, FOX_HEADS, HEAD_DIM)
    fv = fv.reshape(B, T, FOX_HEADS, HEAD_DIM)
    logf = jax.nn.log_sigmoid((ff + fox_fbias[l]).astype(jnp.float32))
    nq = nq.reshape(B, T, NSA_KV_HEADS, NSA_GROUP, HEAD_DIM)
    nkv = nkv.reshape(B, T, 6, NSA_KV_HEADS, HEAD_DIM)
    ng = jax.nn.sigmoid(ng.reshape(B, T, NSA_KV_HEADS, NSA_GROUP, 3))
    dq = dq.reshape(B, T, DIFF_HEADS, 2, HEAD_DIM)
    dk = dk.reshape(B, T, DIFF_HEADS, DIFF_VDIM)
    dv = dv.reshape(B, T, DIFF_HEADS, DIFF_VDIM)
    bg = jax.nn.sigmoid(bg.reshape(B, T, 3, D_MODEL))

    fox_kv_new = jnp.stack([fk, fv], axis=2)
    nsa_kv_new = nkv[:, :, :4]
    win_new = nkv[:, :, 4:]
    diff_kv_new = jnp.stack([dk, dv], axis=2)

    if past is None:
        fox_kv, fox_lf, nsa_kv, diff_kv, win_hist = fox_kv_new, logf, nsa_kv_new, diff_kv_new, win_new
    else:
        p_fkv, p_flf, p_nkv, p_dkv, p_win = past
        fox_kv = jnp.concatenate([p_fkv, fox_kv_new], axis=1)
        fox_lf = jnp.concatenate([p_flf.astype(jnp.float32), logf], axis=1)
        nsa_kv = jnp.concatenate([p_nkv, nsa_kv_new], axis=1)
        diff_kv = jnp.concatenate([p_dkv, diff_kv_new], axis=1)
        win_hist = jnp.concatenate([p_win, win_new], axis=1)
    L = fox_kv.shape[1]
    win_ctx = jnp.pad(win_hist, ((0, 0), (WINDOW + T - win_hist.shape[1], 0), (0, 0), (0, 0), (0, 0)))
    win_state = win_hist[:, win_hist.shape[1] - min(WINDOW, win_hist.shape[1]):]

    o_fox = fox_attend(fq, qpos, fox_kv[:, :, 0], fox_kv[:, :, 1], fox_lf)
    o_nsa = nsa_attend(nq, qpos, nsa_kv[:, :, 0], nsa_kv[:, :, 1], nsa_kv[:, :, 2], nsa_kv[:, :, 3],
                       win_ctx[:, :, 0], win_ctx[:, :, 1], ng, nsa_phi_w1[l], nsa_phi_w2[l], nsa_phi_pos[l],
                       rel_bias[:, :NSA_HEADS])
    lam_init = 0.8 - 0.6 * math.exp(-0.3 * l)
    lp = diff_lambda[l].astype(jnp.float32)
    lam = jnp.exp(jnp.sum(lp[0] * lp[1])) - jnp.exp(jnp.sum(lp[2] * lp[3])) + lam_init
    o_diff = diff_attend(dq, qpos, diff_kv[:, :, 0].reshape(B, L, DIFF_HEADS, 2, HEAD_DIM), diff_kv[:, :, 1],
                         lam, rel_bias[:, NSA_HEADS:])
    o_diff = rms_norm(o_diff, diff_subln_g[l]) * (1.0 - lam_init)

    merged = (bg[:, :, 0] * (o_fox.reshape(B, T, -1) @ w_br_fox[l])
              + bg[:, :, 1] * (o_nsa.reshape(B, T, -1) @ w_br_nsa[l])
              + bg[:, :, 2] * (o_diff.reshape(B, T, -1) @ w_br_diff[l]))
    h = h + rms_norm(merged @ w_o[l], g[3])
    h = h + 0.5 * rms_norm(ffn(rms_norm(h, g[4]), 1), g[5])
    return h, (fox_kv_new, logf, nsa_kv_new, diff_kv_new, win_state)


def setup_inputs(seed: int = 0) -> dict:
    key = jax.random.key(seed)
    ks = jax.random.split(key, 24)
    f32 = jnp.float32
    nrm = lambda k, shape, s: s * jax.random.normal(k, shape, f32)
    n_pages = PAST_LEN // PAGE_SIZE
    n_used = DEC_BATCH * n_pages
    n_phys = n_used + n_used // 4
    fw = FOX_HEADS * HEAD_DIM
    nw = NSA_HEADS * HEAD_DIM
    dw = DIFF_HEADS * DIFF_VDIM
    return {
        'x_prompt': nrm(ks[0], (BATCH, SEQ, D_MODEL), 1.0),
        'x_sample': nrm(ks[1], (DEC_BATCH, DEC_SEQ, D_MODEL), 1.0),
        'cache_fox_kv': nrm(ks[2], (n_phys, DEPTH, PAGE_SIZE, 2, FOX_HEADS, HEAD_DIM), 1.0),
        'cache_fox_logf': jax.nn.log_sigmoid(FORGET_BIAS_INIT + jax.random.normal(ks[3], (n_phys, DEPTH, PAGE_SIZE, FOX_HEADS), f32)),
        'cache_nsa_kv': nrm(ks[4], (n_phys, DEPTH, PAGE_SIZE, 4, NSA_KV_HEADS, HEAD_DIM), 1.0),
        'cache_diff_kv': nrm(ks[5], (n_phys, DEPTH, PAGE_SIZE, 2, DIFF_HEADS, DIFF_VDIM), 1.0),
        'state_nsa_win': nrm(ks[6], (DEC_BATCH, DEPTH, min(WINDOW, PAST_LEN), 2, NSA_KV_HEADS, HEAD_DIM), 1.0),
        'page_table': jax.random.permutation(ks[7], n_phys)[:n_used].reshape(DEC_BATCH, n_pages).astype(jnp.int32),
        'norm_g': 1.0 + nrm(ks[8], (DEPTH, 6, D_MODEL), 0.02),
        'w_in': nrm(ks[9], (DEPTH, D_MODEL, D_IN), D_MODEL ** -0.5),
        'fox_fbias': FORGET_BIAS_INIT + nrm(ks[10], (DEPTH, FOX_HEADS), 0.1),
        'nsa_phi_w1': nrm(ks[11], (DEPTH, 2, CMP_BLOCK, HEAD_DIM, HEAD_DIM), (CMP_BLOCK * HEAD_DIM) ** -0.5),
        'nsa_phi_w2': nrm(ks[12], (DEPTH, 2, HEAD_DIM, HEAD_DIM), HEAD_DIM ** -0.5),
        'nsa_phi_pos': nrm(ks[13], (DEPTH, 2, CMP_BLOCK, HEAD_DIM), 0.1),
        'diff_lambda': nrm(ks[14], (DEPTH, 4, HEAD_DIM), 0.1),
        'diff_subln_g': 1.0 + nrm(ks[15], (DEPTH, DIFF_VDIM), 0.02),
        'w_br_fox': nrm(ks[16], (DEPTH, fw, D_MODEL), fw ** -0.5),
        'w_br_nsa': nrm(ks[17], (DEPTH, nw, D_MODEL), nw ** -0.5),
        'w_br_diff': nrm(ks[18], (DEPTH, dw, D_MODEL), dw ** -0.5),
        'w_o': nrm(ks[19], (DEPTH, D_MODEL, D_MODEL), D_MODEL ** -0.5),
        'ffn_w_gate': nrm(ks[20], (DEPTH, 2, D_MODEL, D_FF), D_MODEL ** -0.5),
        'ffn_w_up': nrm(ks[21], (DEPTH, 2, D_MODEL, D_FF), D_MODEL ** -0.5),
        'ffn_w_down': nrm(ks[22], (DEPTH, 2, D_FF, D_MODEL), D_FF ** -0.5),
        'rel_bias': nrm(ks[23], (N_BUCKETS, NSA_HEADS + DIFF_HEADS), 0.1),
    }


def reference(x_prompt, x_sample, cache_fox_kv, cache_fox_logf, cache_nsa_kv, cache_diff_kv, state_nsa_win,
              page_table, norm_g, w_in, fox_fbias, nsa_phi_w1, nsa_phi_w2, nsa_phi_pos, diff_lambda,
              diff_subln_g, w_br_fox, w_br_nsa, w_br_diff, w_o, ffn_w_gate, ffn_w_up, ffn_w_down, rel_bias):
    weights = (norm_g, w_in, fox_fbias, nsa_phi_w1, nsa_phi_w2, nsa_phi_pos, diff_lambda, diff_subln_g,
               w_br_fox, w_br_nsa, w_br_diff, w_o, ffn_w_gate, ffn_w_up, ffn_w_down, rel_bias)
    qpos_p = jnp.arange(x_prompt.shape[1], dtype=jnp.int32)
    qpos_s = PAST_LEN + jnp.arange(x_sample.shape[1], dtype=jnp.int32)
    hp, hs = x_prompt, x_sample
    st_p, st_s = [], []
    for l in range(DEPTH):
        hp, new_p = trunk_layer(hp, qpos_p, None, l, *weights)
        past = (gather_pages(cache_fox_kv, page_table, l), gather_pages(cache_fox_logf, page_table, l),
                gather_pages(cache_nsa_kv, page_table, l), gather_pages(cache_diff_kv, page_table, l),
                state_nsa_win[:, l])
        hs, new_s = trunk_layer(hs, qpos_s, past, l, *weights)
        st_p.append(new_p)
        st_s.append(new_s)
    fox_kv_p = jnp.stack([s[0] for s in st_p], axis=1)
    fox_kv_s = jnp.stack([s[0] for s in st_s], axis=1)
    fox_logf_p = jnp.stack([s[1] for s in st_p], axis=1)
    fox_logf_s = jnp.stack([s[1] for s in st_s], axis=1)
    nsa_kv_p = jnp.stack([s[2] for s in st_p], axis=1)
    nsa_kv_s = jnp.stack([s[2] for s in st_s], axis=1)
    diff_kv_p = jnp.stack([s[3] for s in st_p], axis=1)
    diff_kv_s = jnp.stack([s[3] for s in st_s], axis=1)
    nsa_win_p = jnp.stack([s[4] for s in st_p], axis=1)
    nsa_win_s = jnp.stack([s[4] for s in st_s], axis=1)
    return (hp, hs, fox_kv_p, fox_kv_s, fox_logf_p, fox_logf_s, nsa_kv_p, nsa_kv_s,
            diff_kv_p, diff_kv_s, nsa_win_p, nsa_win_s)
```

```python
import functools
import math

import jax
import jax.numpy as jnp
from jax import lax
from jax.experimental import pallas as pl
from jax.experimental.pallas import tpu as pltpu

F32 = jnp.float32
BF16 = jnp.bfloat16

D_MODEL = 4096
DEPTH = 2
PAST_LEN = 16384
PAGE_SIZE = 128
HEAD_DIM = 128
FOX_HEADS = 12
NSA_HEADS = 12
NSA_KV_HEADS = 3
NSA_GROUP = NSA_HEADS // NSA_KV_HEADS
DIFF_HEADS = 4
DIFF_VDIM = 2 * HEAD_DIM
CMP_BLOCK = 64
N_SELECT = 16
WINDOW = 512
N_BUCKETS = 32
MAX_EXACT = N_BUCKETS // 2
MAX_DISTANCE = 1024
D_FF = 11008
FORCE_SCORE = 1.0e4
NEG_INF = -1.0e30
EPS = 1.0e-6
SCALE = HEAD_DIM ** -0.5

VMEM_LIMIT_BYTES = 56 * 1024 * 1024
LANES = 128

FW = FOX_HEADS * HEAD_DIM
NKVW = 6 * NSA_KV_HEADS * HEAD_DIM
DW = DIFF_HEADS * DIFF_VDIM
COL_FQ, COL_FK, COL_FV = 0, 12, 24
COL_NQ, COL_NKV = 36, 48
COL_DQ, COL_DK, COL_DV = 66, 74, 82
PACK_W = 90 * LANES
N_FF = FOX_HEADS
N_NG = 3 * NSA_HEADS

ATT_TQ = 256
RB_LEN = 4096


def _cparams(sem):
    return pltpu.CompilerParams(dimension_semantics=sem, vmem_limit_bytes=VMEM_LIMIT_BYTES)


def _rms(x, g):
    return x * lax.rsqrt(jnp.mean(x * x, axis=-1, keepdims=True) + EPS) * g


def _rms_kernel(x_ref, g_ref, o_ref):
    o_ref[...] = _rms(x_ref[...], g_ref[...]).astype(o_ref.dtype)


def rms_cast(x, g, *, tr=256):
    m, d = x.shape
    tr = min(tr, m)
    return pl.pallas_call(
        _rms_kernel,
        grid=(m // tr,),
        in_specs=[pl.BlockSpec((tr, d), lambda i: (i, 0)),
                  pl.BlockSpec((1, d), lambda i: (0, 0))],
        out_specs=pl.BlockSpec((tr, d), lambda i: (i, 0)),
        out_shape=jax.ShapeDtypeStruct((m, d), BF16),
        compiler_params=_cparams(("parallel",)),
        name="rms_cast",
    )(x, g.reshape(1, d))


def _resid_norm_kernel(x_ref, y_ref, gp_ref, gn_ref, h_ref, u_ref, *, coef):
    h = x_ref[...] + coef * _rms(y_ref[...], gp_ref[...])
    h_ref[...] = h
    u_ref[...] = _rms(h, gn_ref[...]).astype(u_ref.dtype)


def _resid_kernel(x_ref, y_ref, gp_ref, h_ref, *, coef):
    h_ref[...] = x_ref[...] + coef * _rms(y_ref[...], gp_ref[...])


def resid_norm(x, y, g_post, g_next, coef, *, tr=256):
    m, d = x.shape
    tr = min(tr, m)
    row = pl.BlockSpec((tr, d), lambda i: (i, 0))
    vec = pl.BlockSpec((1, d), lambda i: (0, 0))
    if g_next is None:
        return pl.pallas_call(
            functools.partial(_resid_kernel, coef=coef),
            grid=(m // tr,),
            in_specs=[row, row, vec],
            out_specs=row,
            out_shape=jax.ShapeDtypeStruct((m, d), F32),
            compiler_params=_cparams(("parallel",)),
            name="resid",
        )(x, y, g_post.reshape(1, d)), None
    return pl.pallas_call(
        functools.partial(_resid_norm_kernel, coef=coef),
        grid=(m // tr,),
        in_specs=[row, row, vec, vec],
        out_specs=[row, row],
        out_shape=[jax.ShapeDtypeStruct((m, d), F32), jax.ShapeDtypeStruct((m, d), BF16)],
        compiler_params=_cparams(("parallel",)),
        name="resid_norm",
    )(x, y, g_post.reshape(1, d), g_next.reshape(1, d))


def _sigmoid(x):
    return 1.0 / (1.0 + jnp.exp(-x))


def _mm_kernel(x_ref, w_ref, o_ref, *, act):
    acc = jnp.dot(x_ref[...], w_ref[...], preferred_element_type=F32)
    if act == "sigmoid":
        acc = _sigmoid(acc)
    o_ref[...] = acc.astype(o_ref.dtype)


def matmul(x, w, *, tm, tn, out_dtype=F32, act=None, name="matmul"):
    m, k = x.shape
    n = w.shape[1]
    tm = min(tm, m)
    return pl.pallas_call(
        functools.partial(_mm_kernel, act=act),
        grid=(m // tm, n // tn),
        in_specs=[pl.BlockSpec((tm, k), lambda i, j: (i, 0)),
                  pl.BlockSpec((k, tn), lambda i, j: (0, j))],
        out_specs=pl.BlockSpec((tm, tn), lambda i, j: (i, j)),
        out_shape=jax.ShapeDtypeStruct((m, n), out_dtype),
        compiler_params=_cparams(("parallel", "parallel")),
        name=name,
    )(x, w)


def _gateup_kernel(x_ref, wg_ref, wu_ref, o_ref):
    x = x_ref[...]
    g = jnp.dot(x, wg_ref[...], preferred_element_type=F32)
    u = jnp.dot(x, wu_ref[...], preferred_element_type=F32)
    o_ref[...] = (g * _sigmoid(g) * u).astype(o_ref.dtype)


def gateup(x, wg, wu, *, tm=1024, tn=256):
    m, k = x.shape
    n = wg.shape[1]
    tm = min(tm, m)
    wspec = pl.BlockSpec((k, tn), lambda i, j: (0, j))
    return pl.pallas_call(
        _gateup_kernel,
        grid=(m // tm, n // tn),
        in_specs=[pl.BlockSpec((tm, k), lambda i, j: (i, 0)), wspec, wspec],
        out_specs=pl.BlockSpec((tm, tn), lambda i, j: (i, j)),
        out_shape=jax.ShapeDtypeStruct((m, n), BF16),
        compiler_params=_cparams(("parallel", "parallel")),
        name="ffn_gateup",
    )(x, wg, wu)


def _small_proj_kernel(x_ref, w_ref, b_ref, o_ref):
    s = jnp.dot(x_ref[...], w_ref[...], preferred_element_type=F32)
    lane = lax.broadcasted_iota(jnp.int32, s.shape, 1)
    z = s + b_ref[...]
    logsig = jnp.minimum(z, 0.0) - jnp.log1p(jnp.exp(-jnp.abs(z)))
    o_ref[...] = jnp.where(lane < N_FF, logsig, _sigmoid(s))


def small_proj(x, w, b, *, tm=1024):
    m, k = x.shape
    tm = min(tm, m)
    return pl.pallas_call(
        _small_proj_kernel,
        grid=(m // tm,),
        in_specs=[pl.BlockSpec((tm, k), lambda i: (i, 0)),
                  pl.BlockSpec((k, LANES), lambda i: (0, 0)),
                  pl.BlockSpec((1, LANES), lambda i: (0, 0))],
        out_specs=pl.BlockSpec((tm, LANES), lambda i: (i, 0)),
        out_shape=jax.ShapeDtypeStruct((m, LANES), F32),
        compiler_params=_cparams(("parallel",)),
        name="small_proj",
    )(x, w, b)


def _merge_kernel(of_ref, on_ref, od_ref, wf_ref, wn_ref, wd_ref, g0_ref, g1_ref, g2_ref, o_ref):
    a = jnp.dot(of_ref[...], wf_ref[...], preferred_element_type=F32)
    b = jnp.dot(on_ref[...], wn_ref[...], preferred_element_type=F32)
    c = jnp.dot(od_ref[...], wd_ref[...], preferred_element_type=F32)
    o_ref[...] = (g0_ref[...] * a + g1_ref[...] * b + g2_ref[...] * c).astype(o_ref.dtype)


def merge_branches(o_fox, o_nsa, o_diff, wf, wn, wd, gates, *, tm=1024, tn=512):
    m = o_fox.shape[0]
    tm = min(tm, m)
    nb = D_MODEL // tn
    xs = lambda kdim: pl.BlockSpec((tm, kdim), lambda i, j: (i, 0))
    ws = lambda kdim: pl.BlockSpec((kdim, tn), lambda i, j: (0, j))
    gs = lambda br: pl.BlockSpec((tm, tn), lambda i, j: (i, br * nb + j))
    return pl.pallas_call(
        _merge_kernel,
        grid=(m // tm, nb),
        in_specs=[xs(FW), xs(FW), xs(DW), ws(FW), ws(FW), ws(DW), gs(0), gs(1), gs(2)],
        out_specs=pl.BlockSpec((tm, tn), lambda i, j: (i, j)),
        out_shape=jax.ShapeDtypeStruct((m, D_MODEL), BF16),
        compiler_params=_cparams(("parallel", "parallel")),
        name="merge_branches",
    )(o_fox, o_nsa, o_diff, wf, wn, wd, gates, gates, gates)


def _dot_nt(a, b):
    return lax.dot_general(a, b, (((1,), (1,)), ((), ())), preferred_element_type=F32)


def _online_update(s, v, m_ref, l_ref, acc_ref):
    m_prev = m_ref[...]
    m_new = jnp.maximum(m_prev, jnp.max(s, axis=-1, keepdims=True))
    p = jnp.exp(s - m_new)
    alpha = jnp.exp(m_prev - m_new)
    l_ref[...] = alpha * l_ref[...] + jnp.sum(p, axis=-1, keepdims=True)
    acc_ref[...] = alpha * acc_ref[...] + jnp.dot(p.astype(BF16), v, preferred_element_type=F32)
    m_ref[...] = m_new


def _init_state(m_ref, l_ref, acc_ref):
    m_ref[...] = jnp.full(m_ref.shape, NEG_INF, F32)
    l_ref[...] = jnp.zeros(l_ref.shape, F32)
    acc_ref[...] = jnp.zeros(acc_ref.shape, F32)


def _causal_tile(tq):
    row = lax.broadcasted_iota(jnp.int32, (tq, tq), 0)
    col = lax.broadcasted_iota(jnp.int32, (tq, tq), 1)
    return col <= row


def _n_bias_tiles(t, tq):
    return min(t // tq, -(-(MAX_DISTANCE + tq - 1) // tq))


def _build_bias_tiles(rb_row, tiles_ref, n_tiles, tq, t, *, lead=()):
    for j in range(n_tiles):
        base = t - j * tq - tq
        win = rb_row(base, base + 2 * tq)
        x = jnp.broadcast_to(win, (tq, 2 * tq))
        y = pltpu.roll(x, tq + 1, 1, stride=1, stride_axis=0)[:, :tq]
        if j == 0:
            y = jnp.where(_causal_tile(tq), y, NEG_INF)
        tiles_ref[lead + (j,)] = y


def _rev_bias(rel_bias_cols, t):
    n = jnp.arange(t, dtype=jnp.int32)
    nf = jnp.maximum(n, 1).astype(F32)
    large = MAX_EXACT + (jnp.log(nf / MAX_EXACT) / math.log(MAX_DISTANCE / MAX_EXACT)
                         * (N_BUCKETS - MAX_EXACT)).astype(jnp.int32)
    bucket = jnp.where(n < MAX_EXACT, n, jnp.minimum(large, N_BUCKETS - 1))
    bv = rel_bias_cols[bucket].T.astype(F32)
    return jnp.pad(bv[:, ::-1], ((0, 0), (0, RB_LEN - t)))


def _cumsum_kernel(x_ref, o_ref, *, t, tc):
    row = lax.broadcasted_iota(jnp.int32, (tc, tc), 0)
    col = lax.broadcasted_iota(jnp.int32, (tc, tc), 1)
    tri = jnp.where(col <= row, 1.0, 0.0).astype(F32)
    carry = jnp.zeros((1, LANES), F32)
    for c in range(t // tc):
        pc = jnp.dot(tri, x_ref[c * tc:(c + 1) * tc, :], preferred_element_type=F32,
                     precision=lax.Precision.HIGHEST) + carry
        o_ref[c * tc:(c + 1) * tc, :] = pc
        carry = pc[tc - 1:tc, :]


def cumsum_time(x3):
    b, t, w = x3.shape
    return pl.pallas_call(
        functools.partial(_cumsum_kernel, t=t, tc=256),
        grid=(b,),
        in_specs=[pl.BlockSpec((None, t, w), lambda i: (i, 0, 0))],
        out_specs=pl.BlockSpec((None, t, w), lambda i: (i, 0, 0)),
        out_shape=jax.ShapeDtypeStruct((b, t, w), F32),
        compiler_params=_cparams(("parallel",)),
        name="logf_cumsum",
    )(x3)


def _fox_kernel(q_ref, k_ref, v_ref, f_ref, o_ref, m_ref, l_ref, acc_ref, *, tq):
    qi = pl.program_id(2)
    q = (q_ref[...] * SCALE).astype(BF16)
    _init_state(m_ref, l_ref, acc_ref)

    def chunk(kj, masked):
        ks = pl.multiple_of(kj * tq, tq)
        k = k_ref[pl.ds(ks, tq), :].astype(BF16)
        v = v_ref[pl.ds(ks, tq), :].astype(BF16)
        s = _dot_nt(q, k) - f_ref[:, pl.ds(ks, tq)]
        if masked:
            s = jnp.where(_causal_tile(tq), s, NEG_INF)
        _online_update(s, v, m_ref, l_ref, acc_ref)

    def body(kj, c):
        chunk(kj, False)
        return c

    lax.fori_loop(0, qi, body, 0)
    chunk(qi, True)
    o_ref[...] = (acc_ref[...] / l_ref[...]).astype(o_ref.dtype)


def fox_prompt(a3, f_rows):
    b, t, _ = a3.shape
    tq = ATT_TQ
    return pl.pallas_call(
        functools.partial(_fox_kernel, tq=tq),
        grid=(b, FOX_HEADS, t // tq),
        in_specs=[pl.BlockSpec((None, tq, HEAD_DIM), lambda bi, h, qi: (bi, qi, COL_FQ + h)),
                  pl.BlockSpec((None, t, HEAD_DIM), lambda bi, h, qi: (bi, 0, COL_FK + h)),
                  pl.BlockSpec((None, t, HEAD_DIM), lambda bi, h, qi: (bi, 0, COL_FV + h)),
                  pl.BlockSpec((None, None, 1, t), lambda bi, h, qi: (bi, h, 0, 0))],
        out_specs=pl.BlockSpec((None, tq, HEAD_DIM), lambda bi, h, qi: (bi, qi, h)),
        out_shape=jax.ShapeDtypeStruct((b, t, FW), BF16),
        scratch_shapes=[pltpu.VMEM((tq, 1), F32), pltpu.VMEM((tq, 1), F32),
                        pltpu.VMEM((tq, HEAD_DIM), F32)],
        compiler_params=_cparams(("parallel", "parallel", "arbitrary")),
        name="fox_prompt",
    )(a3, a3, a3, f_rows)


def _diff_lambda(lam_ref, lam_init):
    lp = lam_ref[...]
    a = jnp.sum(lp[0:1, :] * lp[1:2, :], axis=-1, keepdims=True)
    b = jnp.sum(lp[2:3, :] * lp[3:4, :], axis=-1, keepdims=True)
    return jnp.exp(a) - jnp.exp(b) + lam_init


def _diff_kernel(q_ref, k_ref, v_ref, rb_ref, lam_ref, g_ref, o_ref,
                 tiles_ref, m1, l1, a1, m2, l2, a2, *, tq, t, n_tiles, lam_init):
    bi = pl.program_id(1)
    qi = pl.program_id(2)

    @pl.when((bi == 0) & (qi == 0))
    def _():
        _build_bias_tiles(lambda lo, hi: rb_ref[:, lo:hi], tiles_ref, n_tiles, tq, t)

    q1 = (q_ref[:, :HEAD_DIM] * SCALE).astype(BF16)
    q2 = (q_ref[:, HEAD_DIM:] * SCALE).astype(BF16)
    _init_state(m1, l1, a1)
    _init_state(m2, l2, a2)
    far_bias = rb_ref[:, t - 1 - MAX_DISTANCE:t - MAX_DISTANCE]

    def chunk(kj, bias):
        ks = pl.multiple_of(kj * tq, tq)
        k = k_ref[pl.ds(ks, tq), :].astype(BF16)
        v = v_ref[pl.ds(ks, tq), :].astype(BF16)
        _online_update(_dot_nt(q1, k[:, :HEAD_DIM]) + bias, v, m1, l1, a1)
        _online_update(_dot_nt(q2, k[:, HEAD_DIM:]) + bias, v, m2, l2, a2)

    lo = jnp.maximum(qi - (n_tiles - 1), 0)

    def far_body(kj, c):
        chunk(kj, far_bias)
        return c

    def near_body(kj, c):
        chunk(kj, tiles_ref[qi - kj])
        return c

    lax.fori_loop(0, lo, far_body, 0)
    lax.fori_loop(lo, qi + 1, near_body, 0)
    lam = _diff_lambda(lam_ref, lam_init)
    o = a1[...] / l1[...] - lam * (a2[...] / l2[...])
    o_ref[...] = (_rms(o, g_ref[...]) * (1.0 - lam_init)).astype(o_ref.dtype)


def diff_prompt(a3, rb_diff, lam_p, subln_g, lam_init):
    b, t, _ = a3.shape
    tq = ATT_TQ
    n_tiles = _n_bias_tiles(t, tq)
    w = DIFF_VDIM
    st = lambda d: pltpu.VMEM((tq, d), F32)
    return pl.pallas_call(
        functools.partial(_diff_kernel, tq=tq, t=t, n_tiles=n_tiles, lam_init=lam_init),
        grid=(DIFF_HEADS, b, t // tq),
        in_specs=[pl.BlockSpec((None, tq, w), lambda h, bi, qi: (bi, qi, COL_DQ // 2 + h)),
                  pl.BlockSpec((None, t, w), lambda h, bi, qi: (bi, 0, COL_DK // 2 + h)),
                  pl.BlockSpec((None, t, w), lambda h, bi, qi: (bi, 0, COL_DV // 2 + h)),
                  pl.BlockSpec((None, 1, RB_LEN), lambda h, bi, qi: (h, 0, 0)),
                  pl.BlockSpec((4, HEAD_DIM), lambda h, bi, qi: (0, 0)),
                  pl.BlockSpec((1, w), lambda h, bi, qi: (0, 0))],
        out_specs=pl.BlockSpec((None, tq, w), lambda h, bi, qi: (bi, qi, h)),
        out_shape=jax.ShapeDtypeStruct((b, t, DW), BF16),
        scratch_shapes=[pltpu.VMEM((n_tiles, tq, tq), F32),
                        st(1), st(1), st(w), st(1), st(1), st(w)],
        compiler_params=_cparams(("arbitrary", "arbitrary", "arbitrary")),
        name="diff_prompt",
    )(a3, a3, a3, rb_diff.reshape(DIFF_HEADS, 1, RB_LEN), lam_p, subln_g.reshape(1, w))


def _gelu_tanh(x):
    return 0.5 * x * (1.0 + jnp.tanh(math.sqrt(2.0 / math.pi) * (x + 0.044715 * (x * x * x))))


def _compress_kernel(x_ref, w1_ref, w2_ref, pos_ref, o_ref, *, nb):
    acc = jnp.zeros((nb, HEAD_DIM), F32)
    for l in range(CMP_BLOCK):
        xl = x_ref[pl.ds(l, nb, stride=CMP_BLOCK), :] + pos_ref[l:l + 1, :]
        acc = acc + jnp.dot(xl.astype(BF16), w1_ref[l], preferred_element_type=F32)
    h = _gelu_tanh(acc)
    o_ref[...] = jnp.dot(h.astype(BF16), w2_ref[...], preferred_element_type=F32)


def compress_prompt(a3, w1, w2, pos):
    b, t, _ = a3.shape
    nb = t // CMP_BLOCK
    return pl.pallas_call(
        functools.partial(_compress_kernel, nb=nb),
        grid=(b, 2, NSA_KV_HEADS),
        in_specs=[pl.BlockSpec((None, t, HEAD_DIM), lambda bi, s, k: (bi, 0, COL_NKV + s * NSA_KV_HEADS + k)),
                  pl.BlockSpec((None, CMP_BLOCK, HEAD_DIM, HEAD_DIM), lambda bi, s, k: (s, 0, 0, 0)),
                  pl.BlockSpec((None, HEAD_DIM, HEAD_DIM), lambda bi, s, k: (s, 0, 0)),
                  pl.BlockSpec((None, CMP_BLOCK, HEAD_DIM), lambda bi, s, k: (s, 0, 0))],
        out_specs=pl.BlockSpec((None, None, None, nb, HEAD_DIM), lambda bi, s, k: (bi, s, k, 0, 0)),
        out_shape=jax.ShapeDtypeStruct((b, 2, NSA_KV_HEADS, nb, HEAD_DIM), F32),
        compiler_params=_cparams(("parallel", "parallel", "parallel")),
        name="nsa_compress",
    )(a3, w1, w2, pos)


def _masked_softmax_rows(lc, valid):
    lcm = jnp.where(valid, lc, NEG_INF)
    e = jnp.where(valid, jnp.exp(lcm - jnp.max(lcm, axis=-1, keepdims=True)), 0.0)
    den = jnp.sum(e, axis=-1, keepdims=True)
    return e / jnp.where(den > 0.0, den, 1.0)


def _select_blocks(score, blk, t_pos, n_blk, n_sel):
    cur = t_pos // CMP_BLOCK
    forced = (blk == 0) | (blk == cur) | (blk == cur - 1)
    score = jnp.where(forced, FORCE_SCORE, score)
    score = jnp.where(blk * CMP_BLOCK <= t_pos, score, -1.0)
    rank = jnp.zeros(score.shape, jnp.int32)
    for j in range(n_blk):
        cj = score[:, j:j + 1]
        ahead = (cj > score) | ((cj == score) & (blk > j))
        rank = rank + jnp.where(ahead, 1, 0)
    return rank < n_sel


def _cmp_select_kernel(q_ref, kc_ref, vc_ref, o_ref, sel_ref, *, tq, nb):
    qi = pl.program_id(2)
    blk = lax.broadcasted_iota(jnp.int32, (tq, nb), 1)
    t_pos = qi * tq + lax.broadcasted_iota(jnp.int32, (tq, nb), 0)
    valid = blk * CMP_BLOCK + (CMP_BLOCK - 1) <= t_pos
    kc = kc_ref[...].astype(BF16)
    vc = vc_ref[...].astype(BF16)
    score = jnp.zeros((tq, nb), F32)
    for g in range(NSA_GROUP):
        q = (q_ref[:, g * HEAD_DIM:(g + 1) * HEAD_DIM] * SCALE).astype(BF16)
        pc = _masked_softmax_rows(_dot_nt(q, kc), valid)
        score = score + pc
        o_ref[:, g * HEAD_DIM:(g + 1) * HEAD_DIM] = jnp.dot(pc.astype(BF16), vc, preferred_element_type=F32)
    sel = _select_blocks(score, blk, t_pos, nb, min(N_SELECT, nb))
    sel_ref[...] = jnp.where(sel, 1.0, 0.0)


def cmp_select_prompt(a3, cmp_kv):
    b, t, _ = a3.shape
    tq = ATT_TQ
    nb = t // CMP_BLOCK
    gw = NSA_GROUP * HEAD_DIM
    return pl.pallas_call(
        functools.partial(_cmp_select_kernel, tq=tq, nb=nb),
        grid=(b, NSA_KV_HEADS, t // tq),
        in_specs=[pl.BlockSpec((None, tq, gw), lambda bi, k, qi: (bi, qi, COL_NQ // NSA_GROUP + k)),
                  pl.BlockSpec((None, None, None, nb, HEAD_DIM), lambda bi, k, qi: (bi, 0, k, 0, 0)),
                  pl.BlockSpec((None, None, None, nb, HEAD_DIM), lambda bi, k, qi: (bi, 1, k, 0, 0))],
        out_specs=[pl.BlockSpec((None, tq, gw), lambda bi, k, qi: (bi, qi, k)),
                   pl.BlockSpec((None, None, tq, nb), lambda bi, k, qi: (bi, k, qi, 0))],
        out_shape=[jax.ShapeDtypeStruct((b, t, NSA_HEADS * HEAD_DIM), F32),
                   jax.ShapeDtypeStruct((b, NSA_KV_HEADS, t, nb), F32)],
        compiler_params=_cparams(("parallel", "parallel", "parallel")),
        name="nsa_cmp_select",
    )(a3, cmp_kv, cmp_kv)


def _slc_win_kernel(q_ref, ks_ref, vs_ref, kw_ref, vw_ref, sel_ref, rb_ref, ocmp_ref, gate_ref, o_ref,
                    tiles_ref, ms, ls, accs, mw, lw, accw, *, tq, t, nb, n_tiles):
    k_idx = pl.program_id(0)
    bi = pl.program_id(1)
    qi = pl.program_id(2)
    n_win_tiles = WINDOW // tq + 1

    @pl.when((bi == 0) & (qi == 0))
    def _():
        for g in range(NSA_GROUP):
            _build_bias_tiles(lambda lo, hi, g=g: rb_ref[g:g + 1, lo:hi], tiles_ref, n_tiles, tq, t, lead=(g,))

    qs = [(q_ref[:, g * HEAD_DIM:(g + 1) * HEAD_DIM] * SCALE).astype(BF16) for g in range(NSA_GROUP)]
    for g in range(NSA_GROUP):
        _init_state(ms.at[g], ls.at[g], accs.at[g])
        _init_state(mw.at[g], lw.at[g], accw.at[g])
    selb = sel_ref[...].astype(BF16)
    blk_row = lax.broadcasted_iota(jnp.int32, (nb, tq), 0)
    key_col = lax.broadcasted_iota(jnp.int32, (nb, tq), 1)

    def slc_chunk(kj, bias_of):
        ks0 = pl.multiple_of(kj * tq, tq)
        expand = jnp.where((ks0 + key_col) // CMP_BLOCK == blk_row, 1.0, 0.0).astype(BF16)
        chosen = jnp.dot(selb, expand, preferred_element_type=F32) > 0.5
        k = ks_ref[pl.ds(ks0, tq), :].astype(BF16)
        v = vs_ref[pl.ds(ks0, tq), :].astype(BF16)
        for g in range(NSA_GROUP):
            s = jnp.where(chosen, _dot_nt(qs[g], k) + bias_of(g, kj), NEG_INF)
            _online_update(s, v, ms.at[g], ls.at[g], accs.at[g])

    lo = jnp.maximum(qi - (n_tiles - 1), 0)

    def far_body(kj, c):
        slc_chunk(kj, lambda g, kj: rb_ref[g:g + 1, t - 1 - MAX_DISTANCE:t - MAX_DISTANCE])
        return c

    def near_body(kj, c):
        slc_chunk(kj, lambda g, kj: tiles_ref[g, qi - kj])
        return c

    lax.fori_loop(0, lo, far_body, 0)
    lax.fori_loop(lo, qi + 1, near_body, 0)

    row = lax.broadcasted_iota(jnp.int32, (tq, tq), 0)
    col = lax.broadcasted_iota(jnp.int32, (tq, tq), 1)

    def win_chunk(d):
        ks0 = pl.multiple_of((qi - d) * tq, tq)
        k = kw_ref[pl.ds(ks0, tq), :].astype(BF16)
        v = vw_ref[pl.ds(ks0, tq), :].astype(BF16)
        for g in range(NSA_GROUP):
            s = _dot_nt(qs[g], k) + tiles_ref[g, d]
            if d == n_win_tiles - 1:
                s = jnp.where(d * tq + row - col < WINDOW, s, NEG_INF)
            _online_update(s, v, mw.at[g], lw.at[g], accw.at[g])

    win_chunk(0)
    for d in range(1, n_win_tiles):
        pl.when(qi >= d)(functools.partial(win_chunk, d))

    gates = gate_ref[...]
    lane = lax.broadcasted_iota(jnp.int32, gates.shape, 1)
    for g in range(NSA_GROUP):
        base = N_FF + (k_idx * NSA_GROUP + g) * 3
        gate = lambda j: jnp.sum(jnp.where(lane == base + j, gates, 0.0), axis=-1, keepdims=True)
        sl = slice(g * HEAD_DIM, (g + 1) * HEAD_DIM)
        o = (gate(0) * ocmp_ref[:, sl] + gate(1) * (accs[g] / ls[g]) + gate(2) * (accw[g] / lw[g]))
        o_ref[:, sl] = o.astype(o_ref.dtype)


def slc_win_prompt(a3, sel, rb_nsa, o_cmp, s_act):
    b, t, _ = a3.shape
    tq = ATT_TQ
    nb = t // CMP_BLOCK
    n_tiles = _n_bias_tiles(t, tq)
    assert WINDOW % tq == 0 and WINDOW // tq + 1 <= n_tiles
    gw = NSA_GROUP * HEAD_DIM
    kv = lambda stream: pl.BlockSpec(
        (None, t, HEAD_DIM), lambda k, bi, qi: (bi, 0, COL_NKV + stream * NSA_KV_HEADS + k))
    st = lambda d: pltpu.VMEM((NSA_GROUP, tq, d), F32)
    return pl.pallas_call(
        functools.partial(_slc_win_kernel, tq=tq, t=t, nb=nb, n_tiles=n_tiles),
        grid=(NSA_KV_HEADS, b, t // tq),
        in_specs=[pl.BlockSpec((None, tq, gw), lambda k, bi, qi: (bi, qi, COL_NQ // NSA_GROUP + k)),
                  kv(2), kv(3), kv(4), kv(5),
                  pl.BlockSpec((None, None, tq, nb), lambda k, bi, qi: (bi, k, qi, 0)),
                  pl.BlockSpec((None, NSA_GROUP, RB_LEN), lambda k, bi, qi: (k, 0, 0)),
                  pl.BlockSpec((None, tq, gw), lambda k, bi, qi: (bi, qi, k)),
                  pl.BlockSpec((None, tq, LANES), lambda k, bi, qi: (bi, qi, 0))],
        out_specs=pl.BlockSpec((None, tq, gw), lambda k, bi, qi: (bi, qi, k)),
        out_shape=jax.ShapeDtypeStruct((b, t, NSA_HEADS * HEAD_DIM), BF16),
        scratch_shapes=[pltpu.VMEM((NSA_GROUP, n_tiles, tq, tq), F32),
                        st(1), st(1), st(HEAD_DIM), st(1), st(1), st(HEAD_DIM)],
        compiler_params=_cparams(("arbitrary", "arbitrary", "arbitrary")),
        name="nsa_slc_win",
    )(a3, a3, a3, a3, a3, sel, rb_nsa.reshape(NSA_KV_HEADS, NSA_GROUP, RB_LEN), o_cmp, s_act)


def _masked_softmax(logits, mask):
    p = jax.nn.softmax(jnp.where(mask, logits, NEG_INF), axis=-1)
    return jnp.where(mask, p, 0.0)


def _t5_bucket(dist):
    n = jnp.maximum(dist, 0)
    nf = jnp.maximum(n, 1).astype(F32)
    large = MAX_EXACT + (jnp.log(nf / MAX_EXACT) / math.log(MAX_DISTANCE / MAX_EXACT)
                         * (N_BUCKETS - MAX_EXACT)).astype(jnp.int32)
    return jnp.where(n < MAX_EXACT, n, jnp.minimum(large, N_BUCKETS - 1))


def _gather_pages(pool, page_table, l):
    g = pool[page_table, l]
    return g.reshape((g.shape[0], g.shape[1] * g.shape[2]) + g.shape[3:])


def _sample_attention(a, s_act, past, l, phi_w1, phi_w2, phi_pos, diff_lambda, subln_g, rel_bias):
    bsz = a.shape[0]
    qp = jnp.full((1,), PAST_LEN, jnp.int32)
    p_fkv, p_flf, p_nkv, p_dkv, p_win = past
    fq = a[:, None, 0:FW].reshape(bsz, 1, FOX_HEADS, HEAD_DIM)
    fkv_new = a[:, None, FW:3 * FW].reshape(bsz, 1, 2, FOX_HEADS, HEAD_DIM)
    logf = s_act[:, None, :N_FF]
    ng = s_act[:, None, N_FF:N_FF + N_NG].reshape(bsz, 1, NSA_KV_HEADS, NSA_GROUP, 3)
    nq = a[:, None, COL_NQ * LANES:COL_NKV * LANES].reshape(bsz, 1, NSA_KV_HEADS, NSA_GROUP, HEAD_DIM)
    nkv = a[:, None, COL_NKV * LANES:COL_DQ * LANES].reshape(bsz, 1, 6, NSA_KV_HEADS, HEAD_DIM)
    dq = a[:, None, COL_DQ * LANES:COL_DK * LANES].reshape(bsz, 1, DIFF_HEADS, 2, HEAD_DIM)
    dkv_new = a[:, None, COL_DK * LANES:].reshape(bsz, 1, 2, DIFF_HEADS, DIFF_VDIM)

    fox_kv = jnp.concatenate([p_fkv, fkv_new], axis=1)
    fox_lf = jnp.concatenate([p_flf.astype(F32), logf], axis=1)
    nsa_kv = jnp.concatenate([p_nkv, nkv[:, :, :4]], axis=1)
    diff_kv = jnp.concatenate([p_dkv, dkv_new], axis=1)
    win_hist = jnp.concatenate([p_win, nkv[:, :, 4:]], axis=1)
    big_l = fox_kv.shape[1]
    kpos = jnp.arange(big_l, dtype=jnp.int32)

    cf = jnp.cumsum(fox_lf, axis=1)
    fqv = jnp.take(cf, qp, axis=1)
    logits = jnp.einsum('bthd,bshd->bhts', fq, fox_kv[:, :, 0]).astype(F32) * SCALE
    logits = logits + jnp.moveaxis(fqv, 1, 2)[..., None] - jnp.moveaxis(cf, 1, 2)[:, :, None, :]
    p = _masked_softmax(logits, kpos[None, :] <= qp[:, None])
    o_fox = jnp.einsum('bhts,bshd->bthd', p, fox_kv[:, :, 1])

    lam_init = 0.8 - 0.6 * math.exp(-0.3 * l)
    lp = diff_lambda.astype(F32)
    lam = jnp.exp(jnp.sum(lp[0] * lp[1])) - jnp.exp(jnp.sum(lp[2] * lp[3])) + lam_init
    dist = qp[:, None] - kpos[None, :]
    tbl_d = rel_bias[:, NSA_HEADS:]
    bias = jnp.moveaxis(tbl_d[_t5_bucket(dist)], -1, 0).astype(F32)
    dk = diff_kv[:, :, 0].reshape(bsz, big_l, DIFF_HEADS, 2, HEAD_DIM)
    lg = jnp.einsum('bthcd,bshcd->bhcts', dq, dk).astype(F32) * SCALE + bias[None, :, None]
    aa = _masked_softmax(lg, dist >= 0)
    w = aa[:, :, 0] - lam * aa[:, :, 1]
    o_diff = jnp.einsum('bhts,bshe->bthe', w, diff_kv[:, :, 1])
    o_diff = _rms(o_diff, subln_g) * (1.0 - lam_init)

    tbl = rel_bias[:, :NSA_HEADS]
    nb = -(-big_l // CMP_BLOCK)
    lpad = nb * CMP_BLOCK
    blocks = lambda x: jnp.pad(x, ((0, 0), (0, lpad - big_l), (0, 0), (0, 0))).reshape(
        bsz, nb, CMP_BLOCK, NSA_KV_HEADS, HEAD_DIM)

    def compress(bl, w1, w2, pos):
        h = jax.nn.gelu(jnp.einsum('bnlkd,lde->bnke', bl + pos[:, None, :], w1))
        return jnp.einsum('bnke,ef->bnkf', h, w2)

    k_cmp = compress(blocks(nsa_kv[:, :, 0]), phi_w1[0], phi_w2[0], phi_pos[0])
    v_cmp = compress(blocks(nsa_kv[:, :, 1]), phi_w1[1], phi_w2[1], phi_pos[1])
    ks_b = jnp.moveaxis(blocks(nsa_kv[:, :, 2]), 3, 1)
    vs_b = jnp.moveaxis(blocks(nsa_kv[:, :, 3]), 3, 1)
    blk_id = jnp.arange(nb, dtype=jnp.int32)
    blk_end = blk_id * CMP_BLOCK + CMP_BLOCK - 1
    n_sel = min(N_SELECT, nb)
    tbl_kg = jnp.moveaxis(tbl.reshape(N_BUCKETS, NSA_KV_HEADS, NSA_GROUP), 1, 0)
    b_idx = jnp.arange(bsz)[:, None, None, None]
    g_idx = jnp.arange(NSA_KV_HEADS)[None, :, None, None]
    in_blk = jnp.arange(CMP_BLOCK, dtype=jnp.int32)
    lc = jnp.einsum('btkgd,bnkd->bkgtn', nq, k_cmp).astype(F32) * SCALE
    pc = _masked_softmax(lc, blk_end[None, :] <= qp[:, None])
    o_cmp = jnp.einsum('bkgtn,bnkd->btkgd', pc, v_cmp)
    cur = qp[:, None] // CMP_BLOCK
    forced = (blk_id == 0) | (blk_id == cur) | (blk_id == cur - 1)
    score = jnp.where(forced, FORCE_SCORE, pc.sum(axis=2))
    score = jnp.where(blk_id * CMP_BLOCK <= qp[:, None], score, -1.0)
    _, idx = lax.top_k(score, n_sel)
    k_sel = ks_b[b_idx, g_idx, idx].reshape(bsz, NSA_KV_HEADS, 1, n_sel * CMP_BLOCK, HEAD_DIM)
    v_sel = vs_b[b_idx, g_idx, idx].reshape(bsz, NSA_KV_HEADS, 1, n_sel * CMP_BLOCK, HEAD_DIM)
    spos = (idx[..., None] * CMP_BLOCK + in_blk).reshape(bsz, NSA_KV_HEADS, 1, n_sel * CMP_BLOCK)
    sdist = qp[None, None, :, None] - spos
    sbias = jnp.moveaxis(tbl_kg[g_idx, _t5_bucket(sdist)], -1, 2).astype(F32)
    ls = jnp.einsum('btkgd,bktsd->bkgts', nq, k_sel).astype(F32) * SCALE + sbias
    ps = _masked_softmax(ls, (sdist >= 0)[:, :, None])
    o_slc = jnp.einsum('bkgts,bktsd->btkgd', ps, v_sel)
    wpos = PAST_LEN - WINDOW + jnp.arange(WINDOW + 1, dtype=jnp.int32)
    wdist = qp[:, None] - wpos[None, :]
    wmask = (wdist >= 0) & (wdist < WINDOW) & (wpos >= 0)[None, :]
    wbias = jnp.transpose(tbl[_t5_bucket(wdist)], (2, 0, 1)).reshape(NSA_KV_HEADS, NSA_GROUP, 1, WINDOW + 1)
    lw = jnp.einsum('btkgd,bskd->bkgts', nq, win_hist[:, :, 0]).astype(F32) * SCALE + wbias.astype(F32)
    pw = _masked_softmax(lw, wmask)
    o_win = jnp.einsum('bkgts,bskd->btkgd', pw, win_hist[:, :, 1])
    o_nsa = ng[..., 0:1] * o_cmp + ng[..., 1:2] * o_slc + ng[..., 2:3] * o_win

    win_state = win_hist[:, win_hist.shape[1] - WINDOW:]
    return (o_fox.reshape(bsz, FW).astype(BF16), o_nsa.reshape(bsz, FW).astype(BF16),
            o_diff.reshape(bsz, DW).astype(BF16), win_state)


def _prep_layer_weights(l, w_in, fox_fbias, nsa_phi_w1, nsa_phi_w2, w_br_fox, w_br_nsa, w_br_diff, w_o,
                        ffn_w_gate, ffn_w_up, ffn_w_down):
    wl = w_in[l]
    c0, c1 = 3 * FW, 3 * FW + N_FF
    c2 = c1 + FW + NKVW
    c3 = c2 + N_NG
    c4 = c3 + 3 * DW
    w_pack = jnp.concatenate([wl[:, :c0], wl[:, c1:c2], wl[:, c3:c4]], axis=1).astype(BF16)
    w_small = jnp.concatenate([wl[:, c0:c1], wl[:, c2:c3]], axis=1)
    w_small = jnp.pad(w_small, ((0, 0), (0, LANES - w_small.shape[1]))).astype(BF16)
    b_small = jnp.pad(fox_fbias[l], (0, LANES - N_FF)).reshape(1, LANES).astype(F32)
    return dict(
        w_pack=w_pack, w_small=w_small, b_small=b_small, w_bg=wl[:, c4:].astype(BF16),
        phi_w1=nsa_phi_w1[l].astype(BF16), phi_w2=nsa_phi_w2[l].astype(BF16),
        w_br_fox=w_br_fox[l].astype(BF16), w_br_nsa=w_br_nsa[l].astype(BF16),
        w_br_diff=w_br_diff[l].astype(BF16), w_o=w_o[l].astype(BF16),
        wg=[ffn_w_gate[l, i].astype(BF16) for i in range(2)],
        wu=[ffn_w_up[l, i].astype(BF16) for i in range(2)],
        wd=[ffn_w_down[l, i].astype(BF16) for i in range(2)],
    )


def _ffn(xn, w, i):
    act = gateup(xn, w["wg"][i], w["wu"][i])
    return matmul(act, w["wd"][i], tm=512, tn=512, name="ffn_down")


def _layer(x, xn, w, g, g_next, attend):
    h, u = resid_norm(x, _ffn(xn, w, 0), g[1], g[2], 0.5)
    a = matmul(u, w["w_pack"], tm=1024, tn=768, name="proj_pack")
    s_act = small_proj(u, w["w_small"], w["b_small"])
    gates = matmul(u, w["w_bg"], tm=1024, tn=1024, act="sigmoid", name="proj_gates")
    o_fox, o_nsa, o_diff, extra = attend(a, s_act)
    merged = merge_branches(o_fox, o_nsa, o_diff, w["w_br_fox"], w["w_br_nsa"], w["w_br_diff"], gates)
    z = matmul(merged, w["w_o"], tm=1024, tn=1024, name="proj_out")
    h, u3 = resid_norm(h, z, g[3], g[4], 1.0)
    y, yn = resid_norm(h, _ffn(u3, w, 1), g[5], g_next, 0.5)
    return y, yn, a, s_act, extra


def kernel(x_prompt, x_sample, cache_fox_kv, cache_fox_logf, cache_nsa_kv, cache_diff_kv, state_nsa_win,
           page_table, norm_g, w_in, fox_fbias, nsa_phi_w1, nsa_phi_w2, nsa_phi_pos, diff_lambda,
           diff_subln_g, w_br_fox, w_br_nsa, w_br_diff, w_o, ffn_w_gate, ffn_w_up, ffn_w_down, rel_bias):
    bp, t, d = x_prompt.shape
    bs = x_sample.shape[0]
    ms = 16
    hp = x_prompt.reshape(bp * t, d)
    hs = jnp.pad(x_sample.reshape(bs, d), ((0, ms - bs), (0, 0)))
    rb = _rev_bias(rel_bias, t)
    rb_nsa, rb_diff = rb[:NSA_HEADS], rb[NSA_HEADS:]

    hpn = rms_cast(hp, norm_g[0, 0])
    hsn = rms_cast(hs, norm_g[0, 0])
    st_p, st_s = [], []
    for l in range(DEPTH):
        w = _prep_layer_weights(l, w_in, fox_fbias, nsa_phi_w1, nsa_phi_w2, w_br_fox, w_br_nsa, w_br_diff,
                                w_o, ffn_w_gate, ffn_w_up, ffn_w_down)
        g = norm_g[l]
        g_next = norm_g[l + 1, 0] if l + 1 < DEPTH else None
        lam_init = 0.8 - 0.6 * math.exp(-0.3 * l)

        def attend_prompt(a, s_act):
            a3 = a.reshape(bp, t, PACK_W)
            s3 = s_act.reshape(bp, t, LANES)
            f_all = cumsum_time(s3)
            f_rows = jnp.transpose(f_all[:, :, :N_FF], (0, 2, 1)).reshape(bp, FOX_HEADS, 1, t)
            o_fox = fox_prompt(a3, f_rows)
            o_diff = diff_prompt(a3, rb_diff, diff_lambda[l], diff_subln_g[l], lam_init)
            cmp_kv = compress_prompt(a3, w["phi_w1"], w["phi_w2"], nsa_phi_pos[l])
            o_cmp, sel = cmp_select_prompt(a3, cmp_kv)
            o_nsa = slc_win_prompt(a3, sel, rb_nsa, o_cmp, s3)
            return (o_fox.reshape(bp * t, FW), o_nsa.reshape(bp * t, FW), o_diff.reshape(bp * t, DW), None)

        def attend_sample(a, s_act):
            past = (_gather_pages(cache_fox_kv, page_table, l), _gather_pages(cache_fox_logf, page_table, l),
                    _gather_pages(cache_nsa_kv, page_table, l), _gather_pages(cache_diff_kv, page_table, l),
                    state_nsa_win[:, l])
            o_fox, o_nsa, o_diff, win_state = _sample_attention(
                a[:bs], s_act[:bs], past, l, nsa_phi_w1[l], nsa_phi_w2[l], nsa_phi_pos[l],
                diff_lambda[l], diff_subln_g[l], rel_bias)
            padr = lambda o: jnp.pad(o, ((0, ms - bs), (0, 0)))
            return padr(o_fox), padr(o_nsa), padr(o_diff), win_state

        hp, hpn, a_p, s_p, _ = _layer(hp, hpn, w, g, g_next, attend_prompt)
        hs, hsn, a_s, s_s, win_s = _layer(hs, hsn, w, g, g_next, attend_sample)
        st_p.append((a_p.reshape(bp, t, PACK_W), s_p.reshape(bp, t, LANES)))
        st_s.append((a_s[:bs].reshape(bs, 1, PACK_W), s_s[:bs].reshape(bs, 1, LANES), win_s))

    def states(st, nb_, tt):
        stack = lambda f: jnp.stack([f(s) for s in st], axis=1)
        fox_kv = stack(lambda s: s[0][:, :, FW:3 * FW].reshape(nb_, tt, 2, FOX_HEADS, HEAD_DIM))
        logf = stack(lambda s: s[1][:, :, :N_FF])
        nsa_kv = stack(lambda s: s[0][:, :, COL_NKV * LANES:(COL_NKV + 12) * LANES].reshape(
            nb_, tt, 4, NSA_KV_HEADS, HEAD_DIM))
        diff_kv = stack(lambda s: s[0][:, :, COL_DK * LANES:].reshape(nb_, tt, 2, DIFF_HEADS, DIFF_VDIM))
        return fox_kv, logf, nsa_kv, diff_kv

    fox_kv_p, logf_p, nsa_kv_p, diff_kv_p = states(st_p, bp, t)
    fox_kv_s, logf_s, nsa_kv_s, diff_kv_s = states(st_s, bs, 1)
    win_lo = (COL_NKV + 12) * LANES
    nsa_win_p = jnp.stack([s[0][:, t - min(WINDOW, t):, win_lo:COL_DQ * LANES].reshape(
        bp, min(WINDOW, t), 2, NSA_KV_HEADS, HEAD_DIM) for s in st_p], axis=1)
    nsa_win_s = jnp.stack([s[2] for s in st_s], axis=1)
    return (hp.reshape(bp, t, d), hs[:bs].reshape(bs, 1, d), fox_kv_p, fox_kv_s, logf_p, logf_s,
            nsa_kv_p, nsa_kv_s, diff_kv_p, diff_kv_s, nsa_win_p, nsa_win_s)
```

```python
import functools
import math

import jax
import jax.numpy as jnp
from jax import lax
from jax.experimental import pallas as pl
from jax.experimental.pallas import tpu as pltpu

F32 = jnp.float32
BF16 = jnp.bfloat16

D_MODEL = 4096
DEPTH = 2
PAST_LEN = 16384
PAGE_SIZE = 128
HEAD_DIM = 128
FOX_HEADS = 12
NSA_HEADS = 12
NSA_KV_HEADS = 3
NSA_GROUP = NSA_HEADS // NSA_KV_HEADS
DIFF_HEADS = 4
DIFF_VDIM = 2 * HEAD_DIM
CMP_BLOCK = 64
N_SELECT = 16
WINDOW = 512
N_BUCKETS = 32
MAX_EXACT = N_BUCKETS // 2
MAX_DISTANCE = 1024
D_FF = 11008
FORCE_SCORE = 1.0e4
NEG_INF = -1.0e30
EPS = 1.0e-6
SCALE = HEAD_DIM ** -0.5

VMEM_LIMIT_BYTES = 56 * 1024 * 1024
LANES = 128

FW = FOX_HEADS * HEAD_DIM
NKVW = 6 * NSA_KV_HEADS * HEAD_DIM
DW = DIFF_HEADS * DIFF_VDIM
COL_FQ, COL_FK, COL_FV = 0, 12, 24
COL_NQ, COL_NKV = 36, 48
COL_DQ, COL_DK, COL_DV = 66, 74, 82
PACK_W = 90 * LANES
N_FF = FOX_HEADS
N_NG = 3 * NSA_HEADS

ATT_TQ = 256
RB_LEN = 4096


def _cparams(sem):
    return pltpu.CompilerParams(dimension_semantics=sem, vmem_limit_bytes=VMEM_LIMIT_BYTES)


def _rms(x, g):
    return x * lax.rsqrt(jnp.mean(x * x, axis=-1, keepdims=True) + EPS) * g


def _rms_kernel(x_ref, g_ref, o_ref):
    o_ref[...] = _rms(x_ref[...], g_ref[...]).astype(o_ref.dtype)


def rms_cast(x, g, *, tr=256):
    m, d = x.shape
    tr = min(tr, m)
    return pl.pallas_call(
        _rms_kernel,
        grid=(m // tr,),
        in_specs=[pl.BlockSpec((tr, d), lambda i: (i, 0)),
                  pl.BlockSpec((1, d), lambda i: (0, 0))],
        out_specs=pl.BlockSpec((tr, d), lambda i: (i, 0)),
        out_shape=jax.ShapeDtypeStruct((m, d), BF16),
        compiler_params=_cparams(("parallel",)),
        name="rms_cast",
    )(x, g.reshape(1, d))


def _resid_norm_kernel(x_ref, y_ref, gp_ref, gn_ref, h_ref, u_ref, *, coef):
    h = x_ref[...] + coef * _rms(y_ref[...], gp_ref[...])
    h_ref[...] = h
    u_ref[...] = _rms(h, gn_ref[...]).astype(u_ref.dtype)


def _resid_kernel(x_ref, y_ref, gp_ref, h_ref, *, coef):
    h_ref[...] = x_ref[...] + coef * _rms(y_ref[...], gp_ref[...])


def resid_norm(x, y, g_post, g_next, coef, *, tr=256):
    m, d = x.shape
    tr = min(tr, m)
    row = pl.BlockSpec((tr, d), lambda i: (i, 0))
    vec = pl.BlockSpec((1, d), lambda i: (0, 0))
    if g_next is None:
        return pl.pallas_call(
            functools.partial(_resid_kernel, coef=coef),
            grid=(m // tr,),
            in_specs=[row, row, vec],
            out_specs=row,
            out_shape=jax.ShapeDtypeStruct((m, d), F32),
            compiler_params=_cparams(("parallel",)),
            name="resid",
        )(x, y, g_post.reshape(1, d)), None
    return pl.pallas_call(
        functools.partial(_resid_norm_kernel, coef=coef),
        grid=(m // tr,),
        in_specs=[row, row, vec, vec],
        out_specs=[row, row],
        out_shape=[jax.ShapeDtypeStruct((m, d), F32), jax.ShapeDtypeStruct((m, d), BF16)],
        compiler_params=_cparams(("parallel",)),
        name="resid_norm",
    )(x, y, g_post.reshape(1, d), g_next.reshape(1, d))


def _sigmoid(x):
    return 1.0 / (1.0 + jnp.exp(-x))


def _mm_kernel(x_ref, w_ref, o_ref, *, act):
    acc = jnp.dot(x_ref[...], w_ref[...], preferred_element_type=F32)
    if act == "sigmoid":
        acc = _sigmoid(acc)
    o_ref[...] = acc.astype(o_ref.dtype)


def _weight_spec(w, lead, tn):
    k = w.shape[-2]
    return pl.BlockSpec((None,) * len(lead) + (k, tn), lambda i, j: tuple(lead) + (0, j))


def matmul(x, w, *, tm, tn, out_dtype=F32, act=None, lead=(), name="matmul"):
    m, k = x.shape
    n = w.shape[-1]
    tm = min(tm, m)
    return pl.pallas_call(
        functools.partial(_mm_kernel, act=act),
        grid=(m // tm, n // tn),
        in_specs=[pl.BlockSpec((tm, k), lambda i, j: (i, 0)),
                  _weight_spec(w, lead, tn)],
        out_specs=pl.BlockSpec((tm, tn), lambda i, j: (i, j)),
        out_shape=jax.ShapeDtypeStruct((m, n), out_dtype),
        compiler_params=_cparams(("parallel", "parallel")),
        name=name,
    )(x, w)


def _gateup_kernel(x_ref, wg_ref, wu_ref, o_ref):
    x = x_ref[...]
    g = jnp.dot(x, wg_ref[...], preferred_element_type=F32)
    u = jnp.dot(x, wu_ref[...], preferred_element_type=F32)
    o_ref[...] = (g * _sigmoid(g) * u).astype(o_ref.dtype)


def gateup(x, wg, wu, lead, *, tm=1024, tn=256):
    m, k = x.shape
    n = wg.shape[-1]
    tm = min(tm, m)
    wspec = _weight_spec(wg, lead, tn)
    return pl.pallas_call(
        _gateup_kernel,
        grid=(m // tm, n // tn),
        in_specs=[pl.BlockSpec((tm, k), lambda i, j: (i, 0)), wspec, wspec],
        out_specs=pl.BlockSpec((tm, tn), lambda i, j: (i, j)),
        out_shape=jax.ShapeDtypeStruct((m, n), BF16),
        compiler_params=_cparams(("parallel", "parallel")),
        name="ffn_gateup",
    )(x, wg, wu)


def _small_proj_kernel(x_ref, w_ref, b_ref, o_ref):
    s = jnp.dot(x_ref[...], w_ref[...], preferred_element_type=F32)
    lane = lax.broadcasted_iota(jnp.int32, s.shape, 1)
    z = s + b_ref[...]
    logsig = jnp.minimum(z, 0.0) - jnp.log1p(jnp.exp(-jnp.abs(z)))
    o_ref[...] = jnp.where(lane < N_FF, logsig, _sigmoid(s))


def small_proj(x, w, b, *, tm=1024):
    m, k = x.shape
    tm = min(tm, m)
    return pl.pallas_call(
        _small_proj_kernel,
        grid=(m // tm,),
        in_specs=[pl.BlockSpec((tm, k), lambda i: (i, 0)),
                  pl.BlockSpec((k, LANES), lambda i: (0, 0)),
                  pl.BlockSpec((1, LANES), lambda i: (0, 0))],
        out_specs=pl.BlockSpec((tm, LANES), lambda i: (i, 0)),
        out_shape=jax.ShapeDtypeStruct((m, LANES), F32),
        compiler_params=_cparams(("parallel",)),
        name="small_proj",
    )(x, w, b)


def _merge_kernel(of_ref, on_ref, od_ref, wf_ref, wn_ref, wd_ref, g0_ref, g1_ref, g2_ref, o_ref):
    a = jnp.dot(of_ref[...], wf_ref[...], preferred_element_type=F32)
    b = jnp.dot(on_ref[...], wn_ref[...], preferred_element_type=F32)
    c = jnp.dot(od_ref[...], wd_ref[...], preferred_element_type=F32)
    o_ref[...] = (g0_ref[...] * a + g1_ref[...] * b + g2_ref[...] * c).astype(o_ref.dtype)


def merge_branches(o_fox, o_nsa, o_diff, wf, wn, wd, gates, *, tm=1024, tn=512):
    m = o_fox.shape[0]
    tm = min(tm, m)
    nb = D_MODEL // tn
    xs = lambda kdim: pl.BlockSpec((tm, kdim), lambda i, j: (i, 0))
    ws = lambda kdim: pl.BlockSpec((kdim, tn), lambda i, j: (0, j))
    gs = lambda br: pl.BlockSpec((tm, tn), lambda i, j: (i, br * nb + j))
    return pl.pallas_call(
        _merge_kernel,
        grid=(m // tm, nb),
        in_specs=[xs(FW), xs(FW), xs(DW), ws(FW), ws(FW), ws(DW), gs(0), gs(1), gs(2)],
        out_specs=pl.BlockSpec((tm, tn), lambda i, j: (i, j)),
        out_shape=jax.ShapeDtypeStruct((m, D_MODEL), BF16),
        compiler_params=_cparams(("parallel", "parallel")),
        name="merge_branches",
    )(o_fox, o_nsa, o_diff, wf, wn, wd, gates, gates, gates)


def _dot_nt(a, b):
    return lax.dot_general(a, b, (((1,), (1,)), ((), ())), preferred_element_type=F32)


def _online_update(s, v, m_ref, l_ref, acc_ref):
    m_prev = m_ref[...]
    m_new = jnp.maximum(m_prev, jnp.max(s, axis=-1, keepdims=True))
    p = jnp.exp(s - m_new)
    alpha = jnp.exp(m_prev - m_new)
    l_ref[...] = alpha * l_ref[...] + jnp.sum(p, axis=-1, keepdims=True)
    acc_ref[...] = alpha * acc_ref[...] + jnp.dot(p.astype(BF16), v, preferred_element_type=F32)
    m_ref[...] = m_new


def _init_state(m_ref, l_ref, acc_ref):
    m_ref[...] = jnp.full(m_ref.shape, NEG_INF, F32)
    l_ref[...] = jnp.zeros(l_ref.shape, F32)
    acc_ref[...] = jnp.zeros(acc_ref.shape, F32)


def _causal_tile(tq):
    row = lax.broadcasted_iota(jnp.int32, (tq, tq), 0)
    col = lax.broadcasted_iota(jnp.int32, (tq, tq), 1)
    return col <= row


def _n_bias_tiles(t, tq):
    return min(t // tq, -(-(MAX_DISTANCE + tq - 1) // tq))


def _build_bias_tiles(rb_row, tiles_ref, n_tiles, tq, t, *, lead=()):
    for j in range(n_tiles):
        base = t - j * tq - tq
        win = rb_row(base, base + 2 * tq)
        x = jnp.broadcast_to(win, (tq, 2 * tq))
        y = pltpu.roll(x, tq + 1, 1, stride=1, stride_axis=0)[:, :tq]
        if j == 0:
            y = jnp.where(_causal_tile(tq), y, NEG_INF)
        tiles_ref[lead + (j,)] = y


def _rev_bias(rel_bias_cols, t):
    n = jnp.arange(t, dtype=jnp.int32)
    nf = jnp.maximum(n, 1).astype(F32)
    large = MAX_EXACT + (jnp.log(nf / MAX_EXACT) / math.log(MAX_DISTANCE / MAX_EXACT)
                         * (N_BUCKETS - MAX_EXACT)).astype(jnp.int32)
    bucket = jnp.where(n < MAX_EXACT, n, jnp.minimum(large, N_BUCKETS - 1))
    bv = rel_bias_cols[bucket].T.astype(F32)
    return jnp.pad(bv[:, ::-1], ((0, 0), (0, RB_LEN - t)))


def _cumsum_kernel(x_ref, o_ref, *, t, tc):
    row = lax.broadcasted_iota(jnp.int32, (tc, tc), 0)
    col = lax.broadcasted_iota(jnp.int32, (tc, tc), 1)
    tri = jnp.where(col <= row, 1.0, 0.0).astype(F32)
    carry = jnp.zeros((1, LANES), F32)
    for c in range(t // tc):
        pc = jnp.dot(tri, x_ref[c * tc:(c + 1) * tc, :], preferred_element_type=F32,
                     precision=lax.Precision.HIGHEST) + carry
        o_ref[c * tc:(c + 1) * tc, :] = pc
        carry = pc[tc - 1:tc, :]


def cumsum_time(x3):
    b, t, w = x3.shape
    return pl.pallas_call(
        functools.partial(_cumsum_kernel, t=t, tc=256),
        grid=(b,),
        in_specs=[pl.BlockSpec((None, t, w), lambda i: (i, 0, 0))],
        out_specs=pl.BlockSpec((None, t, w), lambda i: (i, 0, 0)),
        out_shape=jax.ShapeDtypeStruct((b, t, w), F32),
        compiler_params=_cparams(("parallel",)),
        name="logf_cumsum",
    )(x3)


def _fox_kernel(q_ref, k_ref, v_ref, f_ref, o_ref, m_ref, l_ref, acc_ref, *, tq):
    qi = pl.program_id(2)
    q = (q_ref[...] * SCALE).astype(BF16)
    _init_state(m_ref, l_ref, acc_ref)

    def chunk(kj, masked):
        ks = pl.multiple_of(kj * tq, tq)
        k = k_ref[pl.ds(ks, tq), :].astype(BF16)
        v = v_ref[pl.ds(ks, tq), :].astype(BF16)
        s = _dot_nt(q, k) - f_ref[:, pl.ds(ks, tq)]
        if masked:
            s = jnp.where(_causal_tile(tq), s, NEG_INF)
        _online_update(s, v, m_ref, l_ref, acc_ref)

    def body(kj, c):
        chunk(kj, False)
        return c

    lax.fori_loop(0, qi, body, 0)
    chunk(qi, True)
    o_ref[...] = (acc_ref[...] / l_ref[...]).astype(o_ref.dtype)


def fox_prompt(a3, f_rows):
    b, t, _ = a3.shape
    tq = ATT_TQ
    return pl.pallas_call(
        functools.partial(_fox_kernel, tq=tq),
        grid=(b, FOX_HEADS, t // tq),
        in_specs=[pl.BlockSpec((None, tq, HEAD_DIM), lambda bi, h, qi: (bi, qi, COL_FQ + h)),
                  pl.BlockSpec((None, t, HEAD_DIM), lambda bi, h, qi: (bi, 0, COL_FK + h)),
                  pl.BlockSpec((None, t, HEAD_DIM), lambda bi, h, qi: (bi, 0, COL_FV + h)),
                  pl.BlockSpec((None, None, 1, t), lambda bi, h, qi: (bi, h, 0, 0))],
        out_specs=pl.BlockSpec((None, tq, HEAD_DIM), lambda bi, h, qi: (bi, qi, h)),
        out_shape=jax.ShapeDtypeStruct((b, t, FW), BF16),
        scratch_shapes=[pltpu.VMEM((tq, 1), F32), pltpu.VMEM((tq, 1), F32),
                        pltpu.VMEM((tq, HEAD_DIM), F32)],
        compiler_params=_cparams(("parallel", "parallel", "arbitrary")),
        name="fox_prompt",
    )(a3, a3, a3, f_rows)


def _diff_lambda(lam_ref, lam_init):
    lp = lam_ref[...]
    a = jnp.sum(lp[0:1, :] * lp[1:2, :], axis=-1, keepdims=True)
    b = jnp.sum(lp[2:3, :] * lp[3:4, :], axis=-1, keepdims=True)
    return jnp.exp(a) - jnp.exp(b) + lam_init


def _diff_kernel(q_ref, k_ref, v_ref, rb_ref, lam_ref, g_ref, o_ref,
                 tiles_ref, m1, l1, a1, m2, l2, a2, *, tq, t, n_tiles, lam_init):
    bi = pl.program_id(1)
    qi = pl.program_id(2)

    @pl.when((bi == 0) & (qi == 0))
    def _():
        _build_bias_tiles(lambda lo, hi: rb_ref[:, lo:hi], tiles_ref, n_tiles, tq, t)

    q1 = (q_ref[:, :HEAD_DIM] * SCALE).astype(BF16)
    q2 = (q_ref[:, HEAD_DIM:] * SCALE).astype(BF16)
    _init_state(m1, l1, a1)
    _init_state(m2, l2, a2)
    far_bias = rb_ref[:, t - 1 - MAX_DISTANCE:t - MAX_DISTANCE]

    def chunk(kj, bias):
        ks = pl.multiple_of(kj * tq, tq)
        k = k_ref[pl.ds(ks, tq), :].astype(BF16)
        v = v_ref[pl.ds(ks, tq), :].astype(BF16)
        _online_update(_dot_nt(q1, k[:, :HEAD_DIM]) + bias, v, m1, l1, a1)
        _online_update(_dot_nt(q2, k[:, HEAD_DIM:]) + bias, v, m2, l2, a2)

    lo = jnp.maximum(qi - (n_tiles - 1), 0)

    def far_body(kj, c):
        chunk(kj, far_bias)
        return c

    def near_body(kj, c):
        chunk(kj, tiles_ref[qi - kj])
        return c

    lax.fori_loop(0, lo, far_body, 0)
    lax.fori_loop(lo, qi + 1, near_body, 0)
    lam = _diff_lambda(lam_ref, lam_init)
    o = a1[...] / l1[...] - lam * (a2[...] / l2[...])
    o_ref[...] = (_rms(o, g_ref[...]) * (1.0 - lam_init)).astype(o_ref.dtype)


def diff_prompt(a3, rb_diff, lam_p, subln_g, lam_init):
    b, t, _ = a3.shape
    tq = ATT_TQ
    n_tiles = _n_bias_tiles(t, tq)
    w = DIFF_VDIM
    st = lambda d: pltpu.VMEM((tq, d), F32)
    return pl.pallas_call(
        functools.partial(_diff_kernel, tq=tq, t=t, n_tiles=n_tiles, lam_init=lam_init),
        grid=(DIFF_HEADS, b, t // tq),
        in_specs=[pl.BlockSpec((None, tq, w), lambda h, bi, qi: (bi, qi, COL_DQ // 2 + h)),
                  pl.BlockSpec((None, t, w), lambda h, bi, qi: (bi, 0, COL_DK // 2 + h)),
                  pl.BlockSpec((None, t, w), lambda h, bi, qi: (bi, 0, COL_DV // 2 + h)),
                  pl.BlockSpec((None, 1, RB_LEN), lambda h, bi, qi: (h, 0, 0)),
                  pl.BlockSpec((4, HEAD_DIM), lambda h, bi, qi: (0, 0)),
                  pl.BlockSpec((1, w), lambda h, bi, qi: (0, 0))],
        out_specs=pl.BlockSpec((None, tq, w), lambda h, bi, qi: (bi, qi, h)),
        out_shape=jax.ShapeDtypeStruct((b, t, DW), BF16),
        scratch_shapes=[pltpu.VMEM((n_tiles, tq, tq), F32),
                        st(1), st(1), st(w), st(1), st(1), st(w)],
        compiler_params=_cparams(("arbitrary", "arbitrary", "arbitrary")),
        name="diff_prompt",
    )(a3, a3, a3, rb_diff.reshape(DIFF_HEADS, 1, RB_LEN), lam_p, subln_g.reshape(1, w))


def _gelu_tanh(x):
    return 0.5 * x * (1.0 + jnp.tanh(math.sqrt(2.0 / math.pi) * (x + 0.044715 * (x * x * x))))


def _compress_kernel(x_ref, w1_ref, w2_ref, pos_ref, o_ref, *, nb):
    acc = jnp.zeros((nb, HEAD_DIM), F32)
    for l in range(CMP_BLOCK):
        xl = x_ref[pl.ds(l, nb, stride=CMP_BLOCK), :] + pos_ref[l:l + 1, :]
        acc = acc + jnp.dot(xl.astype(BF16), w1_ref[l], preferred_element_type=F32)
    h = _gelu_tanh(acc)
    o_ref[...] = jnp.dot(h.astype(BF16), w2_ref[...], preferred_element_type=F32)


def compress_prompt(a3, w1, w2, pos):
    b, t, _ = a3.shape
    nb = t // CMP_BLOCK
    return pl.pallas_call(
        functools.partial(_compress_kernel, nb=nb),
        grid=(b, 2, NSA_KV_HEADS),
        in_specs=[pl.BlockSpec((None, t, HEAD_DIM), lambda bi, s, k: (bi, 0, COL_NKV + s * NSA_KV_HEADS + k)),
                  pl.BlockSpec((None, CMP_BLOCK, HEAD_DIM, HEAD_DIM), lambda bi, s, k: (s, 0, 0, 0)),
                  pl.BlockSpec((None, HEAD_DIM, HEAD_DIM), lambda bi, s, k: (s, 0, 0)),
                  pl.BlockSpec((None, CMP_BLOCK, HEAD_DIM), lambda bi, s, k: (s, 0, 0))],
        out_specs=pl.BlockSpec((None, None, None, nb, HEAD_DIM), lambda bi, s, k: (bi, s, k, 0, 0)),
        out_shape=jax.ShapeDtypeStruct((b, 2, NSA_KV_HEADS, nb, HEAD_DIM), F32),
        compiler_params=_cparams(("parallel", "parallel", "parallel")),
        name="nsa_compress",
    )(a3, w1, w2, pos)


def _masked_softmax_rows(lc, valid):
    lcm = jnp.where(valid, lc, NEG_INF)
    e = jnp.where(valid, jnp.exp(lcm - jnp.max(lcm, axis=-1, keepdims=True)), 0.0)
    den = jnp.sum(e, axis=-1, keepdims=True)
    return e / jnp.where(den > 0.0, den, 1.0)


def _select_blocks(score, blk, t_pos, n_blk, n_sel):
    cur = t_pos // CMP_BLOCK
    forced = (blk == 0) | (blk == cur) | (blk == cur - 1)
    score = jnp.where(forced, FORCE_SCORE, score)
    score = jnp.where(blk * CMP_BLOCK <= t_pos, score, -1.0)
    rank = jnp.zeros(score.shape, jnp.int32)
    for j in range(n_blk):
        cj = score[:, j:j + 1]
        ahead = (cj > score) | ((cj == score) & (blk > j))
        rank = rank + jnp.where(ahead, 1, 0)
    return rank < n_sel


def _cmp_select_kernel(q_ref, kc_ref, vc_ref, o_ref, sel_ref, *, tq, nb):
    qi = pl.program_id(2)
    blk = lax.broadcasted_iota(jnp.int32, (tq, nb), 1)
    t_pos = qi * tq + lax.broadcasted_iota(jnp.int32, (tq, nb), 0)
    valid = blk * CMP_BLOCK + (CMP_BLOCK - 1) <= t_pos
    kc = kc_ref[...].astype(BF16)
    vc = vc_ref[...].astype(BF16)
    score = jnp.zeros((tq, nb), F32)
    for g in range(NSA_GROUP):
        q = (q_ref[:, g * HEAD_DIM:(g + 1) * HEAD_DIM] * SCALE).astype(BF16)
        pc = _masked_softmax_rows(_dot_nt(q, kc), valid)
        score = score + pc
        o_ref[:, g * HEAD_DIM:(g + 1) * HEAD_DIM] = jnp.dot(pc.astype(BF16), vc, preferred_element_type=F32)
    sel = _select_blocks(score, blk, t_pos, nb, min(N_SELECT, nb))
    sel_ref[...] = jnp.where(sel, 1.0, 0.0)


def cmp_select_prompt(a3, cmp_kv):
    b, t, _ = a3.shape
    tq = ATT_TQ
    nb = t // CMP_BLOCK
    gw = NSA_GROUP * HEAD_DIM
    return pl.pallas_call(
        functools.partial(_cmp_select_kernel, tq=tq, nb=nb),
        grid=(b, NSA_KV_HEADS, t // tq),
        in_specs=[pl.BlockSpec((None, tq, gw), lambda bi, k, qi: (bi, qi, COL_NQ // NSA_GROUP + k)),
                  pl.BlockSpec((None, None, None, nb, HEAD_DIM), lambda bi, k, qi: (bi, 0, k, 0, 0)),
                  pl.BlockSpec((None, None, None, nb, HEAD_DIM), lambda bi, k, qi: (bi, 1, k, 0, 0))],
        out_specs=[pl.BlockSpec((None, tq, gw), lambda bi, k, qi: (bi, qi, k)),
                   pl.BlockSpec((None, None, tq, nb), lambda bi, k, qi: (bi, k, qi, 0))],
        out_shape=[jax.ShapeDtypeStruct((b, t, NSA_HEADS * HEAD_DIM), F32),
                   jax.ShapeDtypeStruct((b, NSA_KV_HEADS, t, nb), F32)],
        compiler_params=_cparams(("parallel", "parallel", "parallel")),
        name="nsa_cmp_select",
    )(a3, cmp_kv, cmp_kv)


def _slc_win_kernel(q_ref, ks_ref, vs_ref, kw_ref, vw_ref, sel_ref, rb_ref, ocmp_ref, gate_ref, o_ref,
                    tiles_ref, ms, ls, accs, mw, lw, accw, *, tq, t, nb, n_tiles):
    k_idx = pl.program_id(0)
    bi = pl.program_id(1)
    qi = pl.program_id(2)
    n_win_tiles = WINDOW // tq + 1

    @pl.when((bi == 0) & (qi == 0))
    def _():
        for g in range(NSA_GROUP):
            _build_bias_tiles(lambda lo, hi, g=g: rb_ref[g:g + 1, lo:hi], tiles_ref, n_tiles, tq, t, lead=(g,))

    qs = [(q_ref[:, g * HEAD_DIM:(g + 1) * HEAD_DIM] * SCALE).astype(BF16) for g in range(NSA_GROUP)]
    for g in range(NSA_GROUP):
        _init_state(ms.at[g], ls.at[g], accs.at[g])
        _init_state(mw.at[g], lw.at[g], accw.at[g])
    selb = sel_ref[...].astype(BF16)
    blk_row = lax.broadcasted_iota(jnp.int32, (nb, tq), 0)
    key_col = lax.broadcasted_iota(jnp.int32, (nb, tq), 1)

    def slc_chunk(kj, bias_of):
        ks0 = pl.multiple_of(kj * tq, tq)
        expand = jnp.where((ks0 + key_col) // CMP_BLOCK == blk_row, 1.0, 0.0).astype(BF16)
        chosen = jnp.dot(selb, expand, preferred_element_type=F32) > 0.5
        k = ks_ref[pl.ds(ks0, tq), :].astype(BF16)
        v = vs_ref[pl.ds(ks0, tq), :].astype(BF16)
        for g in range(NSA_GROUP):
            s = jnp.where(chosen, _dot_nt(qs[g], k) + bias_of(g, kj), NEG_INF)
            _online_update(s, v, ms.at[g], ls.at[g], accs.at[g])

    lo = jnp.maximum(qi - (n_tiles - 1), 0)

    def far_body(kj, c):
        slc_chunk(kj, lambda g, kj: rb_ref[g:g + 1, t - 1 - MAX_DISTANCE:t - MAX_DISTANCE])
        return c

    def near_body(kj, c):
        slc_chunk(kj, lambda g, kj: tiles_ref[g, qi - kj])
        return c

    lax.fori_loop(0, lo, far_body, 0)
    lax.fori_loop(lo, qi + 1, near_body, 0)

    row = lax.broadcasted_iota(jnp.int32, (tq, tq), 0)
    col = lax.broadcasted_iota(jnp.int32, (tq, tq), 1)

    def win_chunk(d):
        ks0 = pl.multiple_of((qi - d) * tq, tq)
        k = kw_ref[pl.ds(ks0, tq), :].astype(BF16)
        v = vw_ref[pl.ds(ks0, tq), :].astype(BF16)
        for g in range(NSA_GROUP):
            s = _dot_nt(qs[g], k) + tiles_ref[g, d]
            if d == n_win_tiles - 1:
                s = jnp.where(d * tq + row - col < WINDOW, s, NEG_INF)
            _online_update(s, v, mw.at[g], lw.at[g], accw.at[g])

    win_chunk(0)
    for d in range(1, n_win_tiles):
        pl.when(qi >= d)(functools.partial(win_chunk, d))

    gates = gate_ref[...]
    lane = lax.broadcasted_iota(jnp.int32, gates.shape, 1)
    for g in range(NSA_GROUP):
        base = N_FF + (k_idx * NSA_GROUP + g) * 3
        gate = lambda j: jnp.sum(jnp.where(lane == base + j, gates, 0.0), axis=-1, keepdims=True)
        sl = slice(g * HEAD_DIM, (g + 1) * HEAD_DIM)
        o = (gate(0) * ocmp_ref[:, sl] + gate(1) * (accs[g] / ls[g]) + gate(2) * (accw[g] / lw[g]))
        o_ref[:, sl] = o.astype(o_ref.dtype)


def slc_win_prompt(a3, sel, rb_nsa, o_cmp, s_act):
    b, t, _ = a3.shape
    tq = ATT_TQ
    nb = t // CMP_BLOCK
    n_tiles = _n_bias_tiles(t, tq)
    assert WINDOW % tq == 0 and WINDOW // tq + 1 <= n_tiles
    gw = NSA_GROUP * HEAD_DIM
    kv = lambda stream: pl.BlockSpec(
        (None, t, HEAD_DIM), lambda k, bi, qi: (bi, 0, COL_NKV + stream * NSA_KV_HEADS + k))
    st = lambda d: pltpu.VMEM((NSA_GROUP, tq, d), F32)
    return pl.pallas_call(
        functools.partial(_slc_win_kernel, tq=tq, t=t, nb=nb, n_tiles=n_tiles),
        grid=(NSA_KV_HEADS, b, t // tq),
        in_specs=[pl.BlockSpec((None, tq, gw), lambda k, bi, qi: (bi, qi, COL_NQ // NSA_GROUP + k)),
                  kv(2), kv(3), kv(4), kv(5),
                  pl.BlockSpec((None, None, tq, nb), lambda k, bi, qi: (bi, k, qi, 0)),
                  pl.BlockSpec((None, NSA_GROUP, RB_LEN), lambda k, bi, qi: (k, 0, 0)),
                  pl.BlockSpec((None, tq, gw), lambda k, bi, qi: (bi, qi, k)),
                  pl.BlockSpec((None, tq, LANES), lambda k, bi, qi: (bi, qi, 0))],
        out_specs=pl.BlockSpec((None, tq, gw), lambda k, bi, qi: (bi, qi, k)),
        out_shape=jax.ShapeDtypeStruct((b, t, NSA_HEADS * HEAD_DIM), BF16),
        scratch_shapes=[pltpu.VMEM((NSA_GROUP, n_tiles, tq, tq), F32),
                        st(1), st(1), st(HEAD_DIM), st(1), st(1), st(HEAD_DIM)],
        compiler_params=_cparams(("arbitrary", "arbitrary", "arbitrary")),
        name="nsa_slc_win",
    )(a3, a3, a3, a3, a3, sel, rb_nsa.reshape(NSA_KV_HEADS, NSA_GROUP, RB_LEN), o_cmp, s_act)


DEC_PAGES_PER_STEP = 4
CMP_PAGES_PER_STEP = 16
NEW_ROWS = 16
QROWS = 8
LF_ROWS = 16
HI = lax.Precision.HIGHEST


def _page_specs(pp, block, layer, tail):
    def spec(jj):
        return pl.BlockSpec(block, lambda b, i, pt: (pt[b, i * pp + jj], layer) + tail)
    return [spec(jj) for jj in range(pp)]


def _new_row_live():
    return lax.broadcasted_iota(jnp.int32, (QROWS, NEW_ROWS), 1) == 0


def _fox_dec_kernel(pt_ref, *refs, pp, n_steps):
    kv_refs, lf_refs = refs[:pp], refs[pp:2 * pp]
    q_ref, kn_ref, vn_ref, lfn_ref, o_ref, m_ref, l_ref, acc_ref, carry_ref = refs[2 * pp:]
    i = pl.program_id(1)

    @pl.when(i == 0)
    def _():
        for h in range(FOX_HEADS):
            _init_state(m_ref.at[h], l_ref.at[h], acc_ref.at[h])
        carry_ref[...] = jnp.zeros(carry_ref.shape, F32)

    row = lax.broadcasted_iota(jnp.int32, (PAGE_SIZE, PAGE_SIZE), 0)
    col = lax.broadcasted_iota(jnp.int32, (PAGE_SIZE, PAGE_SIZE), 1)
    tri_u = jnp.where(row <= col, 1.0, 0.0).astype(F32)
    qs = [(q_ref[h] * SCALE).astype(BF16) for h in range(FOX_HEADS)]
    for jj in range(pp):
        ft = jnp.dot(lf_refs[jj][...], tri_u, preferred_element_type=F32, precision=HI) + carry_ref[...]
        carry_ref[...] = jnp.broadcast_to(ft[:, PAGE_SIZE - 1:PAGE_SIZE], carry_ref.shape)
        for h in range(FOX_HEADS):
            k = kv_refs[jj][:, 0, h, :].astype(BF16)
            v = kv_refs[jj][:, 1, h, :].astype(BF16)
            _online_update(_dot_nt(qs[h], k) - ft[h:h + 1, :], v, m_ref.at[h], l_ref.at[h], acc_ref.at[h])

    @pl.when(i == n_steps - 1)
    def _():
        f_new = carry_ref[...] + lfn_ref[...]
        live = _new_row_live()
        for h in range(FOX_HEADS):
            s = _dot_nt(qs[h], kn_ref[h].astype(BF16)) - f_new[h:h + 1, :NEW_ROWS]
            _online_update(jnp.where(live, s, NEG_INF), vn_ref[h].astype(BF16),
                           m_ref.at[h], l_ref.at[h], acc_ref.at[h])
            o_ref[h] = acc_ref[h] / l_ref[h]


def fox_decode(page_table, cache_kv, lf_t, layer, q, k_new, v_new, lf_new):
    bsz, n_pages = page_table.shape
    pp = DEC_PAGES_PER_STEP
    n_steps = n_pages // pp
    hq = pl.BlockSpec((None, FOX_HEADS, QROWS, HEAD_DIM), lambda b, i, pt: (b, 0, 0, 0))
    hn = pl.BlockSpec((None, FOX_HEADS, NEW_ROWS, HEAD_DIM), lambda b, i, pt: (b, 0, 0, 0))
    st = lambda d: pltpu.VMEM((FOX_HEADS, QROWS, d), F32)
    grid_spec = pltpu.PrefetchScalarGridSpec(
        num_scalar_prefetch=1, grid=(bsz, n_steps),
        in_specs=(_page_specs(pp, (None, None, PAGE_SIZE, 2, FOX_HEADS, HEAD_DIM), layer, (0, 0, 0, 0))
                  + _page_specs(pp, (None, None, LF_ROWS, PAGE_SIZE), layer, (0, 0))
                  + [hq, hn, hn, pl.BlockSpec((None, LF_ROWS, LANES), lambda b, i, pt: (b, 0, 0))]),
        out_specs=hq,
        scratch_shapes=[st(1), st(1), st(HEAD_DIM), pltpu.VMEM((LF_ROWS, LANES), F32)])
    return pl.pallas_call(
        functools.partial(_fox_dec_kernel, pp=pp, n_steps=n_steps),
        grid_spec=grid_spec,
        out_shape=jax.ShapeDtypeStruct((bsz, FOX_HEADS, QROWS, HEAD_DIM), F32),
        compiler_params=_cparams(("parallel", "arbitrary")),
        name="fox_decode",
    )(page_table, *([cache_kv] * pp), *([lf_t] * pp), q, k_new, v_new, lf_new)


def _diff_dec_kernel(pt_ref, *refs, pp, n_steps, lam_init):
    kv_refs, b_refs = refs[:pp], refs[pp:2 * pp]
    q_ref, kn_ref, vn_ref, bn_ref, lam_ref, g_ref, o_ref, m_ref, l_ref, acc_ref = refs[2 * pp:]
    i = pl.program_id(1)

    @pl.when(i == 0)
    def _():
        for h in range(DIFF_HEADS):
            for c in range(2):
                _init_state(m_ref.at[h, c], l_ref.at[h, c], acc_ref.at[h, c])

    qs = [[(q_ref[h, c] * SCALE).astype(BF16) for c in range(2)] for h in range(DIFF_HEADS)]
    for jj in range(pp):
        bias = b_refs[jj][...]
        for h in range(DIFF_HEADS):
            v = kv_refs[jj][:, 1, h, :].astype(BF16)
            for c in range(2):
                k = kv_refs[jj][:, 0, h, c * HEAD_DIM:(c + 1) * HEAD_DIM].astype(BF16)
                _online_update(_dot_nt(qs[h][c], k) + bias[h:h + 1, :], v,
                               m_ref.at[h, c], l_ref.at[h, c], acc_ref.at[h, c])

    @pl.when(i == n_steps - 1)
    def _():
        live = _new_row_live()
        lam = _diff_lambda(lam_ref, lam_init)
        for h in range(DIFF_HEADS):
            v = vn_ref[h].astype(BF16)
            for c in range(2):
                s = _dot_nt(qs[h][c], kn_ref[h, c].astype(BF16)) + bn_ref[h:h + 1, :NEW_ROWS]
                _online_update(jnp.where(live, s, NEG_INF), v, m_ref.at[h, c], l_ref.at[h, c], acc_ref.at[h, c])
            o = acc_ref[h, 0] / l_ref[h, 0] - lam * (acc_ref[h, 1] / l_ref[h, 1])
            o_ref[h] = _rms(o, g_ref[...]) * (1.0 - lam_init)


def diff_decode(page_table, cache_kv, bias_t, layer, q, k_new, v_new, bias_new, lam_p, subln_g, lam_init):
    bsz, n_pages = page_table.shape
    pp = DEC_PAGES_PER_STEP
    n_steps = n_pages // pp
    w = DIFF_VDIM

    def bias_spec(jj):
        return pl.BlockSpec((None, QROWS, PAGE_SIZE), lambda b, i, pt: (i * pp + jj, 0, 0))

    const = lambda shape: pl.BlockSpec(shape, lambda b, i, pt: (0,) * len(shape))
    st = lambda d: pltpu.VMEM((DIFF_HEADS, 2, QROWS, d), F32)
    out_spec = pl.BlockSpec((None, DIFF_HEADS, QROWS, w), lambda b, i, pt: (b, 0, 0, 0))
    grid_spec = pltpu.PrefetchScalarGridSpec(
        num_scalar_prefetch=1, grid=(bsz, n_steps),
        in_specs=(_page_specs(pp, (None, None, PAGE_SIZE, 2, DIFF_HEADS, w), layer, (0, 0, 0, 0))
                  + [bias_spec(jj) for jj in range(pp)]
                  + [pl.BlockSpec((None, DIFF_HEADS, 2, QROWS, HEAD_DIM), lambda b, i, pt: (b, 0, 0, 0, 0)),
                     pl.BlockSpec((None, DIFF_HEADS, 2, NEW_ROWS, HEAD_DIM), lambda b, i, pt: (b, 0, 0, 0, 0)),
                     pl.BlockSpec((None, DIFF_HEADS, NEW_ROWS, w), lambda b, i, pt: (b, 0, 0, 0)),
                     const((QROWS, LANES)), const((4, HEAD_DIM)), const((1, w))]),
        out_specs=out_spec,
        scratch_shapes=[st(1), st(1), st(w)])
    return pl.pallas_call(
        functools.partial(_diff_dec_kernel, pp=pp, n_steps=n_steps, lam_init=lam_init),
        grid_spec=grid_spec,
        out_shape=jax.ShapeDtypeStruct((bsz, DIFF_HEADS, QROWS, w), F32),
        compiler_params=_cparams(("parallel", "arbitrary")),
        name="diff_decode",
    )(page_table, *([cache_kv] * pp), *([bias_t] * pp), q, k_new, v_new, bias_new, lam_p, subln_g.reshape(1, w))


def _cmp_dec_kernel(pt_ref, *refs, pp):
    kv_refs = refs[:pp]
    pos_ref, w1_ref, w2_ref, o_ref, g_ref = refs[pp:]
    rows = pp * PAGE_SIZE
    for s in range(2):
        for k in range(NSA_KV_HEADS):
            for jj in range(pp):
                g_ref[s, pl.ds(k * rows + jj * PAGE_SIZE, PAGE_SIZE), :] = kv_refs[jj][:, s, k, :] + pos_ref[s]
    per_head = 2 * pp
    nrow = NSA_KV_HEADS * per_head
    for s in range(2):
        acc = jnp.zeros((nrow, HEAD_DIM), F32)
        for l in range(CMP_BLOCK):
            xl = g_ref.at[s][pl.ds(l, nrow, stride=CMP_BLOCK), :]
            acc = acc + jnp.dot(xl.astype(BF16), w1_ref[s, l], preferred_element_type=F32)
        out = jnp.dot(_gelu_tanh(acc).astype(BF16), w2_ref[s], preferred_element_type=F32)
        for k in range(NSA_KV_HEADS):
            o_ref[s, k] = out[k * per_head:(k + 1) * per_head, :]


def compress_decode(page_table, cache_kv, layer, pos2, w1, w2):
    bsz, n_pages = page_table.shape
    pp = min(CMP_PAGES_PER_STEP, n_pages)
    per_page = PAGE_SIZE // CMP_BLOCK
    const = lambda shape: pl.BlockSpec(shape, lambda b, i, pt: (0,) * len(shape))
    grid_spec = pltpu.PrefetchScalarGridSpec(
        num_scalar_prefetch=1, grid=(bsz, n_pages // pp),
        in_specs=(_page_specs(pp, (None, None, PAGE_SIZE, 2, NSA_KV_HEADS, HEAD_DIM), layer, (0, 0, 0, 0))
                  + [const((2, PAGE_SIZE, HEAD_DIM)), const((2, CMP_BLOCK, HEAD_DIM, HEAD_DIM)),
                     const((2, HEAD_DIM, HEAD_DIM))]),
        out_specs=pl.BlockSpec((None, 2, NSA_KV_HEADS, per_page * pp, HEAD_DIM), lambda b, i, pt: (b, 0, 0, i, 0)),
        scratch_shapes=[pltpu.VMEM((2, NSA_KV_HEADS * pp * PAGE_SIZE, HEAD_DIM), F32)])
    return pl.pallas_call(
        functools.partial(_cmp_dec_kernel, pp=pp),
        grid_spec=grid_spec,
        out_shape=jax.ShapeDtypeStruct((bsz, 2, NSA_KV_HEADS, per_page * n_pages, HEAD_DIM), F32),
        compiler_params=_cparams(("parallel", "arbitrary")),
        name="nsa_compress_decode",
    )(page_table, *([cache_kv] * pp), pos2, w1, w2)


def _cmp_sel_dec_kernel(q_ref, kc_ref, vc_ref, o_ref, idx_ref, *, n_blk, n_pick):
    q = (q_ref[...] * SCALE).astype(BF16)
    lc = _dot_nt(q, kc_ref[...].astype(BF16))
    e = jnp.exp(lc - jnp.max(lc, axis=-1, keepdims=True))
    pc = e / jnp.sum(e, axis=-1, keepdims=True)
    o_ref[...] = jnp.dot(pc.astype(BF16), vc_ref[...].astype(BF16), preferred_element_type=F32)
    rowi = lax.broadcasted_iota(jnp.int32, (QROWS, n_blk), 0)
    blk = lax.broadcasted_iota(jnp.int32, (QROWS, n_blk), 1)
    score = jnp.broadcast_to(jnp.sum(jnp.where(rowi < NSA_GROUP, pc, 0.0), axis=0, keepdims=True), (QROWS, n_blk))
    score = jnp.where((blk == 0) | (blk == n_blk - 1), FORCE_SCORE, score)
    rank = jnp.zeros((QROWS, n_blk), jnp.int32)
    for j in range(n_blk):
        cj = score[:, j:j + 1]
        rank = rank + jnp.where((cj > score) | ((cj == score) & (blk > j)), 1, 0)
    lane = lax.broadcasted_iota(jnp.int32, (QROWS, LANES), 1)
    blk_f = blk.astype(F32)
    picked = jnp.zeros((QROWS, LANES), F32)
    for r in range(n_pick):
        v = jnp.sum(jnp.where(rank == r, blk_f, 0.0), axis=-1, keepdims=True)
        picked = jnp.where(lane == r, v, picked)
    idx_ref[...] = picked.astype(jnp.int32)


def cmp_select_decode(q8, cmp_kv):
    bsz = q8.shape[0]
    n_blk = cmp_kv.shape[3]
    n_pick = N_SELECT - 1
    assert n_blk >= n_pick
    hspec = pl.BlockSpec((None, None, QROWS, HEAD_DIM), lambda b, k: (b, k, 0, 0))
    return pl.pallas_call(
        functools.partial(_cmp_sel_dec_kernel, n_blk=n_blk, n_pick=n_pick),
        grid=(bsz, NSA_KV_HEADS),
        in_specs=[hspec,
                  pl.BlockSpec((None, None, None, n_blk, HEAD_DIM), lambda b, k: (b, 0, k, 0, 0)),
                  pl.BlockSpec((None, None, None, n_blk, HEAD_DIM), lambda b, k: (b, 1, k, 0, 0))],
        out_specs=[hspec, pl.BlockSpec((None, None, QROWS, LANES), lambda b, k: (b, k, 0, 0))],
        out_shape=[jax.ShapeDtypeStruct((bsz, NSA_KV_HEADS, QROWS, HEAD_DIM), F32),
                   jax.ShapeDtypeStruct((bsz, NSA_KV_HEADS, QROWS, LANES), jnp.int32)],
        compiler_params=_cparams(("parallel", "parallel")),
        name="nsa_cmp_select_decode",
    )(q8, cmp_kv, cmp_kv)


def _slc_dec_kernel(pt_ref, sel_ref, *refs, n_pick):
    kv_refs, b_refs = refs[:NSA_KV_HEADS], refs[NSA_KV_HEADS:2 * NSA_KV_HEADS]
    q_ref, kn_ref, vn_ref, bn_ref, o_ref, m_ref, l_ref, acc_ref = refs[2 * NSA_KV_HEADS:]
    i = pl.program_id(1)

    @pl.when(i == 0)
    def _():
        for k in range(NSA_KV_HEADS):
            _init_state(m_ref.at[k], l_ref.at[k], acc_ref.at[k])

    qs = [(q_ref[k] * SCALE).astype(BF16) for k in range(NSA_KV_HEADS)]
    for k in range(NSA_KV_HEADS):
        ks = kv_refs[k][:, 0, k, :].astype(BF16)
        vs = kv_refs[k][:, 1, k, :].astype(BF16)
        _online_update(_dot_nt(qs[k], ks) + b_refs[k][...], vs, m_ref.at[k], l_ref.at[k], acc_ref.at[k])

    @pl.when(i == n_pick - 1)
    def _():
        live = _new_row_live()
        for k in range(NSA_KV_HEADS):
            s = _dot_nt(qs[k], kn_ref[k].astype(BF16)) + bn_ref[k][:, :NEW_ROWS]
            _online_update(jnp.where(live, s, NEG_INF), vn_ref[k].astype(BF16), m_ref.at[k], l_ref.at[k], acc_ref.at[k])
            o_ref[k] = acc_ref[k] / l_ref[k]


def slc_decode(page_table, sel, cache_kv, layer, q8, k_new, v_new, bias_blk, bias_new):
    bsz = q8.shape[0]
    n_pick = N_SELECT - 1
    per_page = PAGE_SIZE // CMP_BLOCK

    def kv_spec(k):
        def idx(b, i, pt, sel):
            blk = sel[b, k * N_SELECT + i]
            return (pt[b, blk // per_page], layer, blk % per_page, 1, 0, 0)
        return pl.BlockSpec((None, None, CMP_BLOCK, 2, NSA_KV_HEADS, HEAD_DIM), idx)

    def bias_spec(k):
        return pl.BlockSpec((None, None, QROWS, CMP_BLOCK), lambda b, i, pt, sel: (k, sel[b, k * N_SELECT + i], 0, 0))

    hq = pl.BlockSpec((None, NSA_KV_HEADS, QROWS, HEAD_DIM), lambda b, i, pt, sel: (b, 0, 0, 0))
    hn = pl.BlockSpec((None, NSA_KV_HEADS, NEW_ROWS, HEAD_DIM), lambda b, i, pt, sel: (b, 0, 0, 0))
    st = lambda d: pltpu.VMEM((NSA_KV_HEADS, QROWS, d), F32)
    grid_spec = pltpu.PrefetchScalarGridSpec(
        num_scalar_prefetch=2, grid=(bsz, n_pick),
        in_specs=([kv_spec(k) for k in range(NSA_KV_HEADS)] + [bias_spec(k) for k in range(NSA_KV_HEADS)]
                  + [hq, hn, hn, pl.BlockSpec((NSA_KV_HEADS, QROWS, LANES), lambda b, i, pt, sel: (0, 0, 0))]),
        out_specs=hq,
        scratch_shapes=[st(1), st(1), st(HEAD_DIM)])
    return pl.pallas_call(
        functools.partial(_slc_dec_kernel, n_pick=n_pick),
        grid_spec=grid_spec,
        out_shape=jax.ShapeDtypeStruct((bsz, NSA_KV_HEADS, QROWS, HEAD_DIM), F32),
        compiler_params=_cparams(("parallel", "arbitrary")),
        name="nsa_slc_decode",
    )(page_table, sel, *([cache_kv] * NSA_KV_HEADS), *([bias_blk] * NSA_KV_HEADS), q8, k_new, v_new, bias_new)


def _win_dec_kernel(q_ref, st_ref, kn_ref, vn_ref, wb_ref, bn_ref, oc_ref, os_ref, g_ref, o_ref):
    live = _new_row_live()
    for k in range(NSA_KV_HEADS):
        q = (q_ref[k] * SCALE).astype(BF16)
        s = _dot_nt(q, st_ref[:, 0, k, :].astype(BF16)) + wb_ref[k]
        sn = jnp.where(live, _dot_nt(q, kn_ref[k].astype(BF16)) + bn_ref[k][:, :NEW_ROWS], NEG_INF)
        m = jnp.maximum(jnp.max(s, axis=-1, keepdims=True), jnp.max(sn, axis=-1, keepdims=True))
        p = jnp.exp(s - m)
        pn = jnp.exp(sn - m)
        den = jnp.sum(p, axis=-1, keepdims=True) + jnp.sum(pn, axis=-1, keepdims=True)
        ow = (jnp.dot(p.astype(BF16), st_ref[:, 1, k, :].astype(BF16), preferred_element_type=F32)
              + jnp.dot(pn.astype(BF16), vn_ref[k].astype(BF16), preferred_element_type=F32)) / den
        o_ref[k] = g_ref[k, 0] * oc_ref[k] + g_ref[k, 1] * os_ref[k] + g_ref[k, 2] * ow


def win_combine_decode(q8, state, layer, k_new, v_new, bias_win, bias_new, o_cmp, o_slc, gates):
    bsz = q8.shape[0]
    wlen = state.shape[2]
    hq = pl.BlockSpec((None, NSA_KV_HEADS, QROWS, HEAD_DIM), lambda b: (b, 0, 0, 0))
    hn = pl.BlockSpec((None, NSA_KV_HEADS, NEW_ROWS, HEAD_DIM), lambda b: (b, 0, 0, 0))
    return pl.pallas_call(
        _win_dec_kernel,
        grid=(bsz,),
        in_specs=[hq,
                  pl.BlockSpec((None, None, wlen, 2, NSA_KV_HEADS, HEAD_DIM), lambda b: (b, layer, 0, 0, 0, 0)),
                  hn, hn,
                  pl.BlockSpec((NSA_KV_HEADS, QROWS, wlen), lambda b: (0, 0, 0)),
                  pl.BlockSpec((NSA_KV_HEADS, QROWS, LANES), lambda b: (0, 0, 0)),
                  hq, hq,
                  pl.BlockSpec((None, NSA_KV_HEADS, 3, QROWS, HEAD_DIM), lambda b: (b, 0, 0, 0, 0))],
        out_specs=hq,
        out_shape=jax.ShapeDtypeStruct((bsz, NSA_KV_HEADS, QROWS, HEAD_DIM), F32),
        compiler_params=_cparams(("parallel",)),
        name="nsa_win_decode",
    )(q8, state, k_new, v_new, bias_win, bias_new, o_cmp, o_slc, gates)


def _decode_tables(rel_bias, past_len, wlen):
    dist = past_len - jnp.arange(past_len + 1, dtype=jnp.int32)
    nf = jnp.maximum(dist, 1).astype(F32)
    large = MAX_EXACT + (jnp.log(nf / MAX_EXACT) / math.log(MAX_DISTANCE / MAX_EXACT)
                         * (N_BUCKETS - MAX_EXACT)).astype(jnp.int32)
    bucket = jnp.where(dist < MAX_EXACT, dist, jnp.minimum(large, N_BUCKETS - 1))
    bias_all = rel_bias[bucket].astype(F32)
    nsa, dif = bias_all[:, :NSA_HEADS], bias_all[:, NSA_HEADS:]
    n_pages, n_blk = past_len // PAGE_SIZE, past_len // CMP_BLOCK
    pad_rows = lambda x, axis: jnp.pad(x, [(0, QROWS - x.shape[axis]) if a == axis else (0, 0) for a in range(x.ndim)])
    lanes = lambda x: jnp.broadcast_to(x[..., None], x.shape + (LANES,))
    diff_t = pad_rows(jnp.transpose(dif[:past_len].reshape(n_pages, PAGE_SIZE, DIFF_HEADS), (0, 2, 1)), 1)
    diff_new = pad_rows(lanes(dif[past_len]), 0)
    nsa_blk = pad_rows(jnp.transpose(nsa[:past_len].reshape(n_blk, CMP_BLOCK, NSA_KV_HEADS, NSA_GROUP), (2, 0, 3, 1)), 2)
    nsa_new = pad_rows(lanes(nsa[past_len].reshape(NSA_KV_HEADS, NSA_GROUP)), 1)
    wdist = dist[past_len - wlen:past_len]
    win = jnp.where((wdist < WINDOW)[None, None, :],
                    jnp.transpose(nsa[past_len - wlen:past_len].reshape(wlen, NSA_KV_HEADS, NSA_GROUP), (1, 2, 0)),
                    NEG_INF)
    return dict(diff_t=diff_t, diff_new=diff_new, nsa_blk=nsa_blk, nsa_new=nsa_new, nsa_win=pad_rows(win, 1))


def _sample_attention(a, s_act, layer, caches, tables, w, nsa_phi_pos, diff_lambda, subln_g, lam_init):
    page_table, cache_fox_kv, lf_t, cache_nsa_kv, cache_diff_kv, state_nsa_win = caches
    bsz = a.shape[0]
    cols = lambda c0, c1: a[:, c0 * LANES:c1 * LANES]
    rep = lambda x: jnp.broadcast_to(x[..., None, :], x.shape[:-1] + (QROWS, x.shape[-1]))
    row0 = lambda x: jnp.pad(x[..., None, :], [(0, 0)] * (x.ndim - 1) + [(0, NEW_ROWS - 1), (0, 0)])
    lanes = lambda x: jnp.broadcast_to(x[..., None], x.shape + (LANES,))

    heads = lambda c0: cols(c0, c0 + FOX_HEADS).reshape(bsz, FOX_HEADS, HEAD_DIM)
    lf_new = lanes(jnp.pad(s_act[:, :N_FF], ((0, 0), (0, LF_ROWS - N_FF))))
    o_fox = fox_decode(page_table, cache_fox_kv, lf_t, layer, rep(heads(COL_FQ)), row0(heads(COL_FK)),
                       row0(heads(COL_FV)), lf_new)[:, :, 0]

    dq = cols(COL_DQ, COL_DK).reshape(bsz, DIFF_HEADS, 2, HEAD_DIM)
    dk = cols(COL_DK, COL_DV).reshape(bsz, DIFF_HEADS, 2, HEAD_DIM)
    dv = cols(COL_DV, COL_DV + 8).reshape(bsz, DIFF_HEADS, DIFF_VDIM)
    o_diff = diff_decode(page_table, cache_diff_kv, tables["diff_t"], layer, rep(dq), row0(dk), row0(dv),
                         tables["diff_new"], diff_lambda, subln_g, lam_init)[:, :, 0]

    nq = cols(COL_NQ, COL_NKV).reshape(bsz, NSA_KV_HEADS, NSA_GROUP, HEAD_DIM)
    q8 = jnp.pad(nq, ((0, 0), (0, 0), (0, QROWS - NSA_GROUP), (0, 0)))
    nkv = cols(COL_NKV, COL_DQ).reshape(bsz, 6, NSA_KV_HEADS, HEAD_DIM)
    pos2 = jnp.tile(nsa_phi_pos, (1, PAGE_SIZE // CMP_BLOCK, 1))
    cmp_kv = compress_decode(page_table, cache_nsa_kv, layer, pos2, w["phi_w1"], w["phi_w2"])
    o_cmp, picked = cmp_select_decode(q8, cmp_kv)
    sel = jnp.pad(picked[:, :, 0, :N_SELECT - 1], ((0, 0), (0, 0), (0, 1))).reshape(bsz, NSA_KV_HEADS * N_SELECT)
    o_slc = slc_decode(page_table, sel, cache_nsa_kv, layer, q8, row0(nkv[:, 2]), row0(nkv[:, 3]),
                       tables["nsa_blk"], tables["nsa_new"])
    ng = s_act[:, N_FF:N_FF + N_NG].reshape(bsz, NSA_KV_HEADS, NSA_GROUP, 3)
    gates = lanes(jnp.pad(jnp.transpose(ng, (0, 1, 3, 2)), ((0, 0), (0, 0), (0, 0), (0, QROWS - NSA_GROUP))))
    o_nsa = win_combine_decode(q8, state_nsa_win, layer, row0(nkv[:, 4]), row0(nkv[:, 5]), tables["nsa_win"],
                               tables["nsa_new"], o_cmp, o_slc, gates)[:, :, :NSA_GROUP]
    win_new = nkv[:, 4:6].reshape(bsz, 1, 2, NSA_KV_HEADS, HEAD_DIM)
    win_state = jnp.concatenate([state_nsa_win[:, layer, 1:], win_new], axis=1)
    return (o_fox.reshape(bsz, FW).astype(BF16), o_nsa.reshape(bsz, FW).astype(BF16),
            o_diff.reshape(bsz, DW).astype(BF16), win_state)


def _prep_layer_weights(l, w_in, fox_fbias, nsa_phi_w1, nsa_phi_w2, w_br_fox, w_br_nsa, w_br_diff, w_o,
                        ffn_w_gate, ffn_w_up, ffn_w_down):
    wl = w_in[l]
    c0, c1 = 3 * FW, 3 * FW + N_FF
    c2 = c1 + FW + NKVW
    c3 = c2 + N_NG
    c4 = c3 + 3 * DW
    w_pack = jnp.concatenate([wl[:, :c0], wl[:, c1:c2], wl[:, c3:c4]], axis=1).astype(BF16)
    w_small = jnp.concatenate([wl[:, c0:c1], wl[:, c2:c3]], axis=1)
    w_small = jnp.pad(w_small, ((0, 0), (0, LANES - w_small.shape[1]))).astype(BF16)
    b_small = jnp.pad(fox_fbias[l], (0, LANES - N_FF)).reshape(1, LANES).astype(F32)
    return dict(
        w_pack=w_pack, w_small=w_small, b_small=b_small, w_bg=wl[:, c4:].astype(BF16),
        phi_w1=nsa_phi_w1[l].astype(BF16), phi_w2=nsa_phi_w2[l].astype(BF16),
        w_br_fox=w_br_fox[l].astype(BF16), w_br_nsa=w_br_nsa[l].astype(BF16),
        w_br_diff=w_br_diff[l].astype(BF16), w_o=w_o[l].astype(BF16),
        layer=l, wg=ffn_w_gate, wu=ffn_w_up, wd=ffn_w_down,
    )


def _ffn(xn, w, i):
    lead = (w["layer"], i)
    act = gateup(xn, w["wg"], w["wu"], lead)
    return matmul(act, w["wd"], tm=512, tn=512, lead=lead, name="ffn_down")


def _layer(x, xn, w, g, g_next, attend):
    h, u = resid_norm(x, _ffn(xn, w, 0), g[1], g[2], 0.5)
    a = matmul(u, w["w_pack"], tm=1024, tn=768, name="proj_pack")
    s_act = small_proj(u, w["w_small"], w["b_small"])
    gates = matmul(u, w["w_bg"], tm=1024, tn=1024, act="sigmoid", name="proj_gates")
    o_fox, o_nsa, o_diff, extra = attend(a, s_act)
    merged = merge_branches(o_fox, o_nsa, o_diff, w["w_br_fox"], w["w_br_nsa"], w["w_br_diff"], gates)
    z = matmul(merged, w["w_o"], tm=1024, tn=1024, name="proj_out")
    h, u3 = resid_norm(h, z, g[3], g[4], 1.0)
    y, yn = resid_norm(h, _ffn(u3, w, 1), g[5], g_next, 0.5)
    return y, yn, a, s_act, extra


def kernel(x_prompt, x_sample, cache_fox_kv, cache_fox_logf, cache_nsa_kv, cache_diff_kv, state_nsa_win,
           page_table, norm_g, w_in, fox_fbias, nsa_phi_w1, nsa_phi_w2, nsa_phi_pos, diff_lambda,
           diff_subln_g, w_br_fox, w_br_nsa, w_br_diff, w_o, ffn_w_gate, ffn_w_up, ffn_w_down, rel_bias):
    bp, t, d = x_prompt.shape
    bs = x_sample.shape[0]
    ms = 16
    hp = x_prompt.reshape(bp * t, d)
    hs = jnp.pad(x_sample.reshape(bs, d), ((0, ms - bs), (0, 0)))
    rb = _rev_bias(rel_bias, t)
    rb_nsa, rb_diff = rb[:NSA_HEADS], rb[NSA_HEADS:]

    tables = _decode_tables(rel_bias, page_table.shape[1] * PAGE_SIZE, state_nsa_win.shape[2])
    lf_t = jnp.pad(jnp.swapaxes(cache_fox_logf, 2, 3), ((0, 0), (0, 0), (0, LF_ROWS - FOX_HEADS), (0, 0)))
    caches = (page_table, cache_fox_kv, lf_t, cache_nsa_kv, cache_diff_kv, state_nsa_win)

    ffn_w_gate, ffn_w_up, ffn_w_down = (x.astype(BF16) for x in (ffn_w_gate, ffn_w_up, ffn_w_down))

    hpn = rms_cast(hp, norm_g[0, 0])
    hsn = rms_cast(hs, norm_g[0, 0])
    st_p, st_s = [], []
    for l in range(DEPTH):
        w = _prep_layer_weights(l, w_in, fox_fbias, nsa_phi_w1, nsa_phi_w2, w_br_fox, w_br_nsa, w_br_diff,
                                w_o, ffn_w_gate, ffn_w_up, ffn_w_down)
        g = norm_g[l]
        g_next = norm_g[l + 1, 0] if l + 1 < DEPTH else None
        lam_init = 0.8 - 0.6 * math.exp(-0.3 * l)

        def attend_prompt(a, s_act):
            a3 = a.reshape(bp, t, PACK_W)
            s3 = s_act.reshape(bp, t, LANES)
            f_all = cumsum_time(s3)
            f_rows = jnp.transpose(f_all[:, :, :N_FF], (0, 2, 1)).reshape(bp, FOX_HEADS, 1, t)
            o_fox = fox_prompt(a3, f_rows)
            o_diff = diff_prompt(a3, rb_diff, diff_lambda[l], diff_subln_g[l], lam_init)
            cmp_kv = compress_prompt(a3, w["phi_w1"], w["phi_w2"], nsa_phi_pos[l])
            o_cmp, sel = cmp_select_prompt(a3, cmp_kv)
            o_nsa = slc_win_prompt(a3, sel, rb_nsa, o_cmp, s3)
            return (o_fox.reshape(bp * t, FW), o_nsa.reshape(bp * t, FW), o_diff.reshape(bp * t, DW), None)

        def attend_sample(a, s_act):
            o_fox, o_nsa, o_diff, win_state = _sample_attention(
                a[:bs], s_act[:bs], l, caches, tables, w, nsa_phi_pos[l], diff_lambda[l], diff_subln_g[l], lam_init)
            padr = lambda o: jnp.pad(o, ((0, ms - bs), (0, 0)))
            return padr(o_fox), padr(o_nsa), padr(o_diff), win_state

        hp, hpn, a_p, s_p, _ = _layer(hp, hpn, w, g, g_next, attend_prompt)
        hs, hsn, a_s, s_s, win_s = _layer(hs, hsn, w, g, g_next, attend_sample)
        st_p.append((a_p.reshape(bp, t, PACK_W), s_p.reshape(bp, t, LANES)))
        st_s.append((a_s[:bs].reshape(bs, 1, PACK_W), s_s[:bs].reshape(bs, 1, LANES), win_s))

    def states(st, nb_, tt):
        stack = lambda f: jnp.stack([f(s) for s in st], axis=1)
        fox_kv = stack(lambda s: s[0][:, :, FW:3 * FW].reshape(nb_, tt, 2, FOX_HEADS, HEAD_DIM))
        logf = stack(lambda s: s[1][:, :, :N_FF])
        nsa_kv = stack(lambda s: s[0][:, :, COL_NKV * LANES:(COL_NKV + 12) * LANES].reshape(
            nb_, tt, 4, NSA_KV_HEADS, HEAD_DIM))
        diff_kv = stack(lambda s: s[0][:, :, COL_DK * LANES:].reshape(nb_, tt, 2, DIFF_HEADS, DIFF_VDIM))
        return fox_kv, logf, nsa_kv, diff_kv

    fox_kv_p, logf_p, nsa_kv_p, diff_kv_p = states(st_p, bp, t)
    fox_kv_s, logf_s, nsa_kv_s, diff_kv_s = states(st_s, bs, 1)
    win_lo = (COL_NKV + 12) * LANES
    nsa_win_p = jnp.stack([s[0][:, t - min(WINDOW, t):, win_lo:COL_DQ * LANES].reshape(
        bp, min(WINDOW, t), 2, NSA_KV_HEADS, HEAD_DIM) for s in st_p], axis=1)
    nsa_win_s = jnp.stack([s[2] for s in st_s], axis=1)
    return (hp.reshape(bp, t, d), hs[:bs].reshape(bs, 1, d), fox_kv_p, fox_kv_s, logf_p, logf_s,
            nsa_kv_p, nsa_kv_s, diff_kv_p, diff_kv_s, nsa_win_p, nsa_win_s)
```

```python
import functools
import math

import jax
import jax.numpy as jnp
from jax import lax
from jax.experimental import pallas as pl
from jax.experimental.pallas import tpu as pltpu

F32 = jnp.float32
BF16 = jnp.bfloat16

D_MODEL = 4096
DEPTH = 2
PAST_LEN = 16384
PAGE_SIZE = 128
HEAD_DIM = 128
FOX_HEADS = 12
NSA_HEADS = 12
NSA_KV_HEADS = 3
NSA_GROUP = NSA_HEADS // NSA_KV_HEADS
DIFF_HEADS = 4
DIFF_VDIM = 2 * HEAD_DIM
CMP_BLOCK = 64
N_SELECT = 16
WINDOW = 512
N_BUCKETS = 32
MAX_EXACT = N_BUCKETS // 2
MAX_DISTANCE = 1024
D_FF = 11008
FORCE_SCORE = 1.0e4
NEG_INF = -1.0e30
EPS = 1.0e-6
SCALE = HEAD_DIM ** -0.5

VMEM_LIMIT_BYTES = 56 * 1024 * 1024
LANES = 128

FW = FOX_HEADS * HEAD_DIM
NKVW = 6 * NSA_KV_HEADS * HEAD_DIM
DW = DIFF_HEADS * DIFF_VDIM
COL_FQ, COL_FK, COL_FV = 0, 12, 24
COL_NQ, COL_NKV = 36, 48
COL_DQ, COL_DK, COL_DV = 66, 74, 82
PACK_W = 90 * LANES
N_FF = FOX_HEADS
N_NG = 3 * NSA_HEADS

FOX_TQ = 1024
DIFF_TQ = 512
NSA_TQ = 512
RB_LEN = 4096


def _cparams(sem):
    return pltpu.CompilerParams(dimension_semantics=sem, vmem_limit_bytes=VMEM_LIMIT_BYTES)


def _rms(x, g):
    return x * lax.rsqrt(jnp.mean(x * x, axis=-1, keepdims=True) + EPS) * g


def _rms_kernel(x_ref, g_ref, o_ref):
    o_ref[...] = _rms(x_ref[...], g_ref[...]).astype(o_ref.dtype)


def rms_cast(x, g, *, tr=256):
    m, d = x.shape
    tr = min(tr, m)
    return pl.pallas_call(
        _rms_kernel,
        grid=(m // tr,),
        in_specs=[pl.BlockSpec((tr, d), lambda i: (i, 0)),
                  pl.BlockSpec((1, d), lambda i: (0, 0))],
        out_specs=pl.BlockSpec((tr, d), lambda i: (i, 0)),
        out_shape=jax.ShapeDtypeStruct((m, d), BF16),
        compiler_params=_cparams(("parallel",)),
        name="rms_cast",
    )(x, g.reshape(1, d))


def _resid_norm_kernel(x_ref, y_ref, gp_ref, gn_ref, h_ref, u_ref, *, coef):
    h = x_ref[...] + coef * _rms(y_ref[...], gp_ref[...])
    h_ref[...] = h
    u_ref[...] = _rms(h, gn_ref[...]).astype(u_ref.dtype)


def _resid_kernel(x_ref, y_ref, gp_ref, h_ref, *, coef):
    h_ref[...] = x_ref[...] + coef * _rms(y_ref[...], gp_ref[...])


def resid_norm(x, y, g_post, g_next, coef, *, tr=256):
    m, d = x.shape
    tr = min(tr, m)
    row = pl.BlockSpec((tr, d), lambda i: (i, 0))
    vec = pl.BlockSpec((1, d), lambda i: (0, 0))
    if g_next is None:
        return pl.pallas_call(
            functools.partial(_resid_kernel, coef=coef),
            grid=(m // tr,),
            in_specs=[row, row, vec],
            out_specs=row,
            out_shape=jax.ShapeDtypeStruct((m, d), F32),
            compiler_params=_cparams(("parallel",)),
            name="resid",
        )(x, y, g_post.reshape(1, d)), None
    return pl.pallas_call(
        functools.partial(_resid_norm_kernel, coef=coef),
        grid=(m // tr,),
        in_specs=[row, row, vec, vec],
        out_specs=[row, row],
        out_shape=[jax.ShapeDtypeStruct((m, d), F32), jax.ShapeDtypeStruct((m, d), BF16)],
        compiler_params=_cparams(("parallel",)),
        name="resid_norm",
    )(x, y, g_post.reshape(1, d), g_next.reshape(1, d))


def _sigmoid(x):
    return 1.0 / (1.0 + jnp.exp(-x))


def _mm_kernel(x_ref, w_ref, o_ref, *, act):
    acc = jnp.dot(x_ref[...], w_ref[...], preferred_element_type=F32)
    if act == "sigmoid":
        acc = _sigmoid(acc)
    o_ref[...] = acc.astype(o_ref.dtype)


def _weight_spec(w, lead, tn):
    k = w.shape[-2]
    return pl.BlockSpec((None,) * len(lead) + (k, tn), lambda i, j: tuple(lead) + (0, j))


def matmul(x, w, *, tm, tn, out_dtype=F32, act=None, lead=(), name="matmul"):
    m, k = x.shape
    n = w.shape[-1]
    tm = min(tm, m)
    return pl.pallas_call(
        functools.partial(_mm_kernel, act=act),
        grid=(m // tm, n // tn),
        in_specs=[pl.BlockSpec((tm, k), lambda i, j: (i, 0)),
                  _weight_spec(w, lead, tn)],
        out_specs=pl.BlockSpec((tm, tn), lambda i, j: (i, j)),
        out_shape=jax.ShapeDtypeStruct((m, n), out_dtype),
        compiler_params=_cparams(("parallel", "parallel")),
        name=name,
    )(x, w)


def _gateup_kernel(x_ref, wg_ref, wu_ref, o_ref):
    x = x_ref[...]
    g = jnp.dot(x, wg_ref[...], preferred_element_type=F32)
    u = jnp.dot(x, wu_ref[...], preferred_element_type=F32)
    o_ref[...] = (g * _sigmoid(g) * u).astype(o_ref.dtype)


def gateup(x, wg, wu, lead, *, tm=1024, tn=256):
    m, k = x.shape
    n = wg.shape[-1]
    tm = min(tm, m)
    wspec = _weight_spec(wg, lead, tn)
    return pl.pallas_call(
        _gateup_kernel,
        grid=(m // tm, n // tn),
        in_specs=[pl.BlockSpec((tm, k), lambda i, j: (i, 0)), wspec, wspec],
        out_specs=pl.BlockSpec((tm, tn), lambda i, j: (i, j)),
        out_shape=jax.ShapeDtypeStruct((m, n), BF16),
        compiler_params=_cparams(("parallel", "parallel")),
        name="ffn_gateup",
    )(x, wg, wu)


def _small_proj_kernel(x_ref, w_ref, b_ref, o_ref):
    s = jnp.dot(x_ref[...], w_ref[...], preferred_element_type=F32)
    lane = lax.broadcasted_iota(jnp.int32, s.shape, 1)
    z = s + b_ref[...]
    logsig = jnp.minimum(z, 0.0) - jnp.log1p(jnp.exp(-jnp.abs(z)))
    o_ref[...] = jnp.where(lane < N_FF, logsig, _sigmoid(s))


def small_proj(x, w, b, *, tm=1024):
    m, k = x.shape
    tm = min(tm, m)
    return pl.pallas_call(
        _small_proj_kernel,
        grid=(m // tm,),
        in_specs=[pl.BlockSpec((tm, k), lambda i: (i, 0)),
                  pl.BlockSpec((k, LANES), lambda i: (0, 0)),
                  pl.BlockSpec((1, LANES), lambda i: (0, 0))],
        out_specs=pl.BlockSpec((tm, LANES), lambda i: (i, 0)),
        out_shape=jax.ShapeDtypeStruct((m, LANES), F32),
        compiler_params=_cparams(("parallel",)),
        name="small_proj",
    )(x, w, b)


def _merge_kernel(of_ref, on_ref, od_ref, wf_ref, wn_ref, wd_ref, g0_ref, g1_ref, g2_ref, o_ref):
    a = jnp.dot(of_ref[...], wf_ref[...], preferred_element_type=F32)
    b = jnp.dot(on_ref[...], wn_ref[...], preferred_element_type=F32)
    c = jnp.dot(od_ref[...], wd_ref[...], preferred_element_type=F32)
    o_ref[...] = (g0_ref[...] * a + g1_ref[...] * b + g2_ref[...] * c).astype(o_ref.dtype)


def merge_branches(o_fox, o_nsa, o_diff, wf, wn, wd, gates, *, tm=1024, tn=512):
    m = o_fox.shape[0]
    tm = min(tm, m)
    nb = D_MODEL // tn
    xs = lambda kdim: pl.BlockSpec((tm, kdim), lambda i, j: (i, 0))
    ws = lambda kdim: pl.BlockSpec((kdim, tn), lambda i, j: (0, j))
    gs = lambda br: pl.BlockSpec((tm, tn), lambda i, j: (i, br * nb + j))
    return pl.pallas_call(
        _merge_kernel,
        grid=(m // tm, nb),
        in_specs=[xs(FW), xs(FW), xs(DW), ws(FW), ws(FW), ws(DW), gs(0), gs(1), gs(2)],
        out_specs=pl.BlockSpec((tm, tn), lambda i, j: (i, j)),
        out_shape=jax.ShapeDtypeStruct((m, D_MODEL), BF16),
        compiler_params=_cparams(("parallel", "parallel")),
        name="merge_branches",
    )(o_fox, o_nsa, o_diff, wf, wn, wd, gates, gates, gates)


def _dot_nt(a, b):
    return lax.dot_general(a, b, (((1,), (1,)), ((), ())), preferred_element_type=F32)


def _online_update(s, v, m_ref, l_ref, acc_ref):
    m_prev = m_ref[...]
    m_new = jnp.maximum(m_prev, jnp.max(s, axis=-1, keepdims=True))
    p = jnp.exp(s - m_new)
    alpha = jnp.exp(m_prev - m_new)
    l_ref[...] = alpha * l_ref[...] + jnp.sum(p, axis=-1, keepdims=True)
    acc_ref[...] = alpha * acc_ref[...] + jnp.dot(p.astype(BF16), v, preferred_element_type=F32)
    m_ref[...] = m_new


def _init_state(m_ref, l_ref, acc_ref):
    m_ref[...] = jnp.full(m_ref.shape, NEG_INF, F32)
    l_ref[...] = jnp.zeros(l_ref.shape, F32)
    acc_ref[...] = jnp.zeros(acc_ref.shape, F32)


def _causal_tile(tq):
    row = lax.broadcasted_iota(jnp.int32, (tq, tq), 0)
    col = lax.broadcasted_iota(jnp.int32, (tq, tq), 1)
    return col <= row


def _n_bias_tiles(t, tq):
    return min(t // tq, -(-(MAX_DISTANCE + tq - 1) // tq))


def _build_bias_tiles(rb_row, tiles_ref, n_tiles, tq, t, *, lead=()):
    for j in range(n_tiles):
        base = t - j * tq - tq
        win = rb_row(base, base + 2 * tq)
        x = jnp.broadcast_to(win, (tq, 2 * tq))
        y = pltpu.roll(x, tq + 1, 1, stride=1, stride_axis=0)[:, :tq]
        if j == 0:
            y = jnp.where(_causal_tile(tq), y, NEG_INF)
        tiles_ref[lead + (j,)] = y


def _rev_bias(rel_bias_cols, t):
    n = jnp.arange(t, dtype=jnp.int32)
    nf = jnp.maximum(n, 1).astype(F32)
    large = MAX_EXACT + (jnp.log(nf / MAX_EXACT) / math.log(MAX_DISTANCE / MAX_EXACT)
                         * (N_BUCKETS - MAX_EXACT)).astype(jnp.int32)
    bucket = jnp.where(n < MAX_EXACT, n, jnp.minimum(large, N_BUCKETS - 1))
    bv = rel_bias_cols[bucket].T.astype(F32)
    return jnp.pad(bv[:, ::-1], ((0, 0), (0, RB_LEN - t)))


def _cumsum_kernel(x_ref, o_ref, *, t, tc):
    row = lax.broadcasted_iota(jnp.int32, (tc, tc), 0)
    col = lax.broadcasted_iota(jnp.int32, (tc, tc), 1)
    tri = jnp.where(col <= row, 1.0, 0.0).astype(F32)
    carry = jnp.zeros((1, LANES), F32)
    for c in range(t // tc):
        pc = jnp.dot(tri, x_ref[c * tc:(c + 1) * tc, :], preferred_element_type=F32,
                     precision=lax.Precision.HIGHEST) + carry
        o_ref[c * tc:(c + 1) * tc, :] = pc
        carry = pc[tc - 1:tc, :]


def cumsum_time(x3):
    b, t, w = x3.shape
    return pl.pallas_call(
        functools.partial(_cumsum_kernel, t=t, tc=256),
        grid=(b,),
        in_specs=[pl.BlockSpec((None, t, w), lambda i: (i, 0, 0))],
        out_specs=pl.BlockSpec((None, t, w), lambda i: (i, 0, 0)),
        out_shape=jax.ShapeDtypeStruct((b, t, w), F32),
        compiler_params=_cparams(("parallel",)),
        name="logf_cumsum",
    )(x3)


def _fox_kernel(q_ref, k_ref, v_ref, f_ref, o_ref, m_ref, l_ref, acc_ref, *, tq):
    qi = pl.program_id(2)
    q = (q_ref[...] * SCALE).astype(BF16)
    _init_state(m_ref, l_ref, acc_ref)

    def chunk(kj, masked):
        ks = pl.multiple_of(kj * tq, tq)
        k = k_ref[pl.ds(ks, tq), :].astype(BF16)
        v = v_ref[pl.ds(ks, tq), :].astype(BF16)
        s = _dot_nt(q, k) - f_ref[:, pl.ds(ks, tq)]
        if masked:
            s = jnp.where(_causal_tile(tq), s, NEG_INF)
        _online_update(s, v, m_ref, l_ref, acc_ref)

    def body(kj, c):
        chunk(kj, False)
        return c

    lax.fori_loop(0, qi, body, 0)
    chunk(qi, True)
    o_ref[...] = (acc_ref[...] / l_ref[...]).astype(o_ref.dtype)


def fox_prompt(a3, f_rows):
    b, t, _ = a3.shape
    tq = FOX_TQ
    return pl.pallas_call(
        functools.partial(_fox_kernel, tq=tq),
        grid=(b, FOX_HEADS, t // tq),
        in_specs=[pl.BlockSpec((None, tq, HEAD_DIM), lambda bi, h, qi: (bi, qi, COL_FQ + h)),
                  pl.BlockSpec((None, t, HEAD_DIM), lambda bi, h, qi: (bi, 0, COL_FK + h)),
                  pl.BlockSpec((None, t, HEAD_DIM), lambda bi, h, qi: (bi, 0, COL_FV + h)),
                  pl.BlockSpec((None, None, 1, t), lambda bi, h, qi: (bi, h, 0, 0))],
        out_specs=pl.BlockSpec((None, tq, HEAD_DIM), lambda bi, h, qi: (bi, qi, h)),
        out_shape=jax.ShapeDtypeStruct((b, t, FW), BF16),
        scratch_shapes=[pltpu.VMEM((tq, 1), F32), pltpu.VMEM((tq, 1), F32),
                        pltpu.VMEM((tq, HEAD_DIM), F32)],
        compiler_params=_cparams(("parallel", "parallel", "arbitrary")),
        name="fox_prompt",
    )(a3, a3, a3, f_rows)


def _diff_lambda(lam_ref, lam_init):
    lp = lam_ref[...]
    a = jnp.sum(lp[0:1, :] * lp[1:2, :], axis=-1, keepdims=True)
    b = jnp.sum(lp[2:3, :] * lp[3:4, :], axis=-1, keepdims=True)
    return jnp.exp(a) - jnp.exp(b) + lam_init


def _diff_kernel(q_ref, k_ref, v_ref, rb_ref, lam_ref, g_ref, o_ref,
                 tiles_ref, m1, l1, a1, m2, l2, a2, *, tq, t, n_tiles, lam_init):
    bi = pl.program_id(1)
    qi = pl.program_id(2)

    @pl.when((bi == 0) & (qi == 0))
    def _():
        _build_bias_tiles(lambda lo, hi: rb_ref[:, lo:hi], tiles_ref, n_tiles, tq, t)

    q1 = (q_ref[:, :HEAD_DIM] * SCALE).astype(BF16)
    q2 = (q_ref[:, HEAD_DIM:] * SCALE).astype(BF16)
    _init_state(m1, l1, a1)
    _init_state(m2, l2, a2)
    far_bias = rb_ref[:, t - 1 - MAX_DISTANCE:t - MAX_DISTANCE]

    def chunk(kj, bias):
        ks = pl.multiple_of(kj * tq, tq)
        k = k_ref[pl.ds(ks, tq), :].astype(BF16)
        v = v_ref[pl.ds(ks, tq), :].astype(BF16)
        _online_update(_dot_nt(q1, k[:, :HEAD_DIM]) + bias, v, m1, l1, a1)
        _online_update(_dot_nt(q2, k[:, HEAD_DIM:]) + bias, v, m2, l2, a2)

    lo = jnp.maximum(qi - (n_tiles - 1), 0)

    def far_body(kj, c):
        chunk(kj, far_bias)
        return c

    def near_body(kj, c):
        chunk(kj, tiles_ref[qi - kj])
        return c

    lax.fori_loop(0, lo, far_body, 0)
    lax.fori_loop(lo, qi + 1, near_body, 0)
    lam = _diff_lambda(lam_ref, lam_init)
    o = a1[...] / l1[...] - lam * (a2[...] / l2[...])
    o_ref[...] = (_rms(o, g_ref[...]) * (1.0 - lam_init)).astype(o_ref.dtype)


def diff_prompt(a3, rb_diff, lam_p, subln_g, lam_init):
    b, t, _ = a3.shape
    tq = DIFF_TQ
    n_tiles = _n_bias_tiles(t, tq)
    w = DIFF_VDIM
    st = lambda d: pltpu.VMEM((tq, d), F32)
    return pl.pallas_call(
        functools.partial(_diff_kernel, tq=tq, t=t, n_tiles=n_tiles, lam_init=lam_init),
        grid=(DIFF_HEADS, b, t // tq),
        in_specs=[pl.BlockSpec((None, tq, w), lambda h, bi, qi: (bi, qi, COL_DQ // 2 + h)),
                  pl.BlockSpec((None, t, w), lambda h, bi, qi: (bi, 0, COL_DK // 2 + h)),
                  pl.BlockSpec((None, t, w), lambda h, bi, qi: (bi, 0, COL_DV // 2 + h)),
                  pl.BlockSpec((None, 1, RB_LEN), lambda h, bi, qi: (h, 0, 0)),
                  pl.BlockSpec((4, HEAD_DIM), lambda h, bi, qi: (0, 0)),
                  pl.BlockSpec((1, w), lambda h, bi, qi: (0, 0))],
        out_specs=pl.BlockSpec((None, tq, w), lambda h, bi, qi: (bi, qi, h)),
        out_shape=jax.ShapeDtypeStruct((b, t, DW), BF16),
        scratch_shapes=[pltpu.VMEM((n_tiles, tq, tq), F32),
                        st(1), st(1), st(w), st(1), st(1), st(w)],
        compiler_params=_cparams(("arbitrary", "arbitrary", "arbitrary")),
        name="diff_prompt",
    )(a3, a3, a3, rb_diff.reshape(DIFF_HEADS, 1, RB_LEN), lam_p, subln_g.reshape(1, w))


def _gelu_tanh(x):
    return 0.5 * x * (1.0 + jnp.tanh(math.sqrt(2.0 / math.pi) * (x + 0.044715 * (x * x * x))))


def _compress_kernel(x_ref, w1_ref, w2_ref, pos_ref, o_ref, *, nb):
    acc = jnp.zeros((nb, HEAD_DIM), F32)
    for l in range(CMP_BLOCK):
        xl = x_ref[pl.ds(l, nb, stride=CMP_BLOCK), :] + pos_ref[l:l + 1, :]
        acc = acc + jnp.dot(xl.astype(BF16), w1_ref[l], preferred_element_type=F32)
    h = _gelu_tanh(acc)
    o_ref[...] = jnp.dot(h.astype(BF16), w2_ref[...], preferred_element_type=F32)


def compress_prompt(a3, w1, w2, pos):
    b, t, _ = a3.shape
    nb = t // CMP_BLOCK
    return pl.pallas_call(
        functools.partial(_compress_kernel, nb=nb),
        grid=(b, 2, NSA_KV_HEADS),
        in_specs=[pl.BlockSpec((None, t, HEAD_DIM), lambda bi, s, k: (bi, 0, COL_NKV + s * NSA_KV_HEADS + k)),
                  pl.BlockSpec((None, CMP_BLOCK, HEAD_DIM, HEAD_DIM), lambda bi, s, k: (s, 0, 0, 0)),
                  pl.BlockSpec((None, HEAD_DIM, HEAD_DIM), lambda bi, s, k: (s, 0, 0)),
                  pl.BlockSpec((None, CMP_BLOCK, HEAD_DIM), lambda bi, s, k: (s, 0, 0))],
        out_specs=pl.BlockSpec((None, None, None, nb, HEAD_DIM), lambda bi, s, k: (bi, s, k, 0, 0)),
        out_shape=jax.ShapeDtypeStruct((b, 2, NSA_KV_HEADS, nb, HEAD_DIM), F32),
        compiler_params=_cparams(("parallel", "parallel", "parallel")),
        name="nsa_compress",
    )(a3, w1, w2, pos)


def _masked_softmax_rows(lc, valid):
    lcm = jnp.where(valid, lc, NEG_INF)
    e = jnp.where(valid, jnp.exp(lcm - jnp.max(lcm, axis=-1, keepdims=True)), 0.0)
    den = jnp.sum(e, axis=-1, keepdims=True)
    return e / jnp.where(den > 0.0, den, 1.0)


def _select_blocks(score, blk, t_pos, n_blk, n_sel):
    cur = t_pos // CMP_BLOCK
    forced = (blk == 0) | (blk == cur) | (blk == cur - 1)
    score = jnp.where(forced, FORCE_SCORE, score)
    score = jnp.where(blk * CMP_BLOCK <= t_pos, score, -1.0)
    rank = jnp.zeros(score.shape, jnp.int32)
    for j in range(n_blk):
        cj = score[:, j:j + 1]
        ahead = (cj > score) | ((cj == score) & (blk > j))
        rank = rank + jnp.where(ahead, 1, 0)
    return rank < n_sel


def _cmp_select_kernel(q_ref, kc_ref, vc_ref, o_ref, sel_ref, *, tq, nb):
    qi = pl.program_id(2)
    blk = lax.broadcasted_iota(jnp.int32, (tq, nb), 1)
    t_pos = qi * tq + lax.broadcasted_iota(jnp.int32, (tq, nb), 0)
    valid = blk * CMP_BLOCK + (CMP_BLOCK - 1) <= t_pos
    kc = kc_ref[...].astype(BF16)
    vc = vc_ref[...].astype(BF16)
    score = jnp.zeros((tq, nb), F32)
    for g in range(NSA_GROUP):
        q = (q_ref[:, g * HEAD_DIM:(g + 1) * HEAD_DIM] * SCALE).astype(BF16)
        pc = _masked_softmax_rows(_dot_nt(q, kc), valid)
        score = score + pc
        o_ref[:, g * HEAD_DIM:(g + 1) * HEAD_DIM] = jnp.dot(pc.astype(BF16), vc, preferred_element_type=F32)
    sel = _select_blocks(score, blk, t_pos, nb, min(N_SELECT, nb))
    sel_ref[...] = jnp.where(sel, 1.0, 0.0)


def cmp_select_prompt(a3, cmp_kv):
    b, t, _ = a3.shape
    tq = NSA_TQ
    nb = t // CMP_BLOCK
    gw = NSA_GROUP * HEAD_DIM
    return pl.pallas_call(
        functools.partial(_cmp_select_kernel, tq=tq, nb=nb),
        grid=(b, NSA_KV_HEADS, t // tq),
        in_specs=[pl.BlockSpec((None, tq, gw), lambda bi, k, qi: (bi, qi, COL_NQ // NSA_GROUP + k)),
                  pl.BlockSpec((None, None, None, nb, HEAD_DIM), lambda bi, k, qi: (bi, 0, k, 0, 0)),
                  pl.BlockSpec((None, None, None, nb, HEAD_DIM), lambda bi, k, qi: (bi, 1, k, 0, 0))],
        out_specs=[pl.BlockSpec((None, tq, gw), lambda bi, k, qi: (bi, qi, k)),
                   pl.BlockSpec((None, None, tq, nb), lambda bi, k, qi: (bi, k, qi, 0))],
        out_shape=[jax.ShapeDtypeStruct((b, t, NSA_HEADS * HEAD_DIM), F32),
                   jax.ShapeDtypeStruct((b, NSA_KV_HEADS, t, nb), F32)],
        compiler_params=_cparams(("parallel", "parallel", "parallel")),
        name="nsa_cmp_select",
    )(a3, cmp_kv, cmp_kv)


def _slc_win_kernel(q_ref, ks_ref, vs_ref, kw_ref, vw_ref, sel_ref, rb_ref, ocmp_ref, gate_ref, o_ref,
                    tiles_ref, ms, ls, accs, mw, lw, accw, *, tq, t, nb, n_tiles):
    k_idx = pl.program_id(0)
    bi = pl.program_id(1)
    qi = pl.program_id(2)
    n_win_tiles = WINDOW // tq + 1

    @pl.when((bi == 0) & (qi == 0))
    def _():
        for g in range(NSA_GROUP):
            _build_bias_tiles(lambda lo, hi, g=g: rb_ref[g:g + 1, lo:hi], tiles_ref, n_tiles, tq, t, lead=(g,))

    qs = [(q_ref[:, g * HEAD_DIM:(g + 1) * HEAD_DIM] * SCALE).astype(BF16) for g in range(NSA_GROUP)]
    for g in range(NSA_GROUP):
        _init_state(ms.at[g], ls.at[g], accs.at[g])
        _init_state(mw.at[g], lw.at[g], accw.at[g])
    selb = sel_ref[...].astype(BF16)
    blk_row = lax.broadcasted_iota(jnp.int32, (nb, tq), 0)
    key_col = lax.broadcasted_iota(jnp.int32, (nb, tq), 1)

    def slc_chunk(kj, bias_of):
        ks0 = pl.multiple_of(kj * tq, tq)
        expand = jnp.where((ks0 + key_col) // CMP_BLOCK == blk_row, 1.0, 0.0).astype(BF16)
        chosen = jnp.dot(selb, expand, preferred_element_type=F32) > 0.5
        k = ks_ref[pl.ds(ks0, tq), :].astype(BF16)
        v = vs_ref[pl.ds(ks0, tq), :].astype(BF16)
        for g in range(NSA_GROUP):
            s = jnp.where(chosen, _dot_nt(qs[g], k) + bias_of(g, kj), NEG_INF)
            _online_update(s, v, ms.at[g], ls.at[g], accs.at[g])

    lo = jnp.maximum(qi - (n_tiles - 1), 0)

    def far_body(kj, c):
        slc_chunk(kj, lambda g, kj: rb_ref[g:g + 1, t - 1 - MAX_DISTANCE:t - MAX_DISTANCE])
        return c

    def near_body(kj, c):
        slc_chunk(kj, lambda g, kj: tiles_ref[g, qi - kj])
        return c

    lax.fori_loop(0, lo, far_body, 0)
    lax.fori_loop(lo, qi + 1, near_body, 0)

    row = lax.broadcasted_iota(jnp.int32, (tq, tq), 0)
    col = lax.broadcasted_iota(jnp.int32, (tq, tq), 1)

    def win_chunk(d):
        ks0 = pl.multiple_of((qi - d) * tq, tq)
        k = kw_ref[pl.ds(ks0, tq), :].astype(BF16)
        v = vw_ref[pl.ds(ks0, tq), :].astype(BF16)
        for g in range(NSA_GROUP):
            s = _dot_nt(qs[g], k) + tiles_ref[g, d]
            if d == n_win_tiles - 1:
                s = jnp.where(d * tq + row - col < WINDOW, s, NEG_INF)
            _online_update(s, v, mw.at[g], lw.at[g], accw.at[g])

    win_chunk(0)
    for d in range(1, n_win_tiles):
        pl.when(qi >= d)(functools.partial(win_chunk, d))

    gates = gate_ref[...]
    lane = lax.broadcasted_iota(jnp.int32, gates.shape, 1)
    for g in range(NSA_GROUP):
        base = N_FF + (k_idx * NSA_GROUP + g) * 3
        gate = lambda j: jnp.sum(jnp.where(lane == base + j, gates, 0.0), axis=-1, keepdims=True)
        sl = slice(g * HEAD_DIM, (g + 1) * HEAD_DIM)
        o = (gate(0) * ocmp_ref[:, sl] + gate(1) * (accs[g] / ls[g]) + gate(2) * (accw[g] / lw[g]))
        o_ref[:, sl] = o.astype(o_ref.dtype)


def slc_win_prompt(a3, sel, rb_nsa, o_cmp, s_act):
    b, t, _ = a3.shape
    tq = NSA_TQ
    nb = t // CMP_BLOCK
    n_tiles = _n_bias_tiles(t, tq)
    assert WINDOW % tq == 0 and WINDOW // tq + 1 <= n_tiles
    gw = NSA_GROUP * HEAD_DIM
    kv = lambda stream: pl.BlockSpec(
        (None, t, HEAD_DIM), lambda k, bi, qi: (bi, 0, COL_NKV + stream * NSA_KV_HEADS + k))
    st = lambda d: pltpu.VMEM((NSA_GROUP, tq, d), F32)
    return pl.pallas_call(
        functools.partial(_slc_win_kernel, tq=tq, t=t, nb=nb, n_tiles=n_tiles),
        grid=(NSA_KV_HEADS, b, t // tq),
        in_specs=[pl.BlockSpec((None, tq, gw), lambda k, bi, qi: (bi, qi, COL_NQ // NSA_GROUP + k)),
                  kv(2), kv(3), kv(4), kv(5),
                  pl.BlockSpec((None, None, tq, nb), lambda k, bi, qi: (bi, k, qi, 0)),
                  pl.BlockSpec((None, NSA_GROUP, RB_LEN), lambda k, bi, qi: (k, 0, 0)),
                  pl.BlockSpec((None, tq, gw), lambda k, bi, qi: (bi, qi, k)),
                  pl.BlockSpec((None, tq, LANES), lambda k, bi, qi: (bi, qi, 0))],
        out_specs=pl.BlockSpec((None, tq, gw), lambda k, bi, qi: (bi, qi, k)),
        out_shape=jax.ShapeDtypeStruct((b, t, NSA_HEADS * HEAD_DIM), BF16),
        scratch_shapes=[pltpu.VMEM((NSA_GROUP, n_tiles, tq, tq), F32),
                        st(1), st(1), st(HEAD_DIM), st(1), st(1), st(HEAD_DIM)],
        compiler_params=_cparams(("arbitrary", "arbitrary", "arbitrary")),
        name="nsa_slc_win",
    )(a3, a3, a3, a3, a3, sel, rb_nsa.reshape(NSA_KV_HEADS, NSA_GROUP, RB_LEN), o_cmp, s_act)


DEC_PAGES_PER_STEP = 8
CMP_PAGES_PER_STEP = 16
NEW_ROWS = 16
QROWS = 8
LF_ROWS = 16
HI = lax.Precision.HIGHEST


def _page_specs(pp, block, layer, tail):
    def spec(jj):
        return pl.BlockSpec(block, lambda b, i, pt: (pt[b, i * pp + jj], layer) + tail)
    return [spec(jj) for jj in range(pp)]


def _new_row_live():
    return lax.broadcasted_iota(jnp.int32, (QROWS, NEW_ROWS), 1) == 0


def _fox_dec_kernel(pt_ref, *refs, pp, n_steps):
    kv_refs, lf_refs = refs[:pp], refs[pp:2 * pp]
    q_ref, kn_ref, vn_ref, lfn_ref, o_ref, m_ref, l_ref, acc_ref, carry_ref = refs[2 * pp:]
    i = pl.program_id(1)

    @pl.when(i == 0)
    def _():
        for h in range(FOX_HEADS):
            _init_state(m_ref.at[h], l_ref.at[h], acc_ref.at[h])
        carry_ref[...] = jnp.zeros(carry_ref.shape, F32)

    row = lax.broadcasted_iota(jnp.int32, (PAGE_SIZE, PAGE_SIZE), 0)
    col = lax.broadcasted_iota(jnp.int32, (PAGE_SIZE, PAGE_SIZE), 1)
    tri_u = jnp.where(row <= col, 1.0, 0.0).astype(F32)
    qs = [(q_ref[h] * SCALE).astype(BF16) for h in range(FOX_HEADS)]
    carry = carry_ref[...]
    fts = []
    for jj in range(pp):
        ft = jnp.dot(lf_refs[jj][...], tri_u, preferred_element_type=F32, precision=HI) + carry
        carry = jnp.broadcast_to(ft[:, PAGE_SIZE - 1:PAGE_SIZE], carry_ref.shape)
        fts.append(ft)
    carry_ref[...] = carry
    f_all = jnp.concatenate(fts, axis=1)
    for h in range(FOX_HEADS):
        rows = lambda jj, kv: kv_refs[jj][pl.ds(2 * h + kv, PAGE_SIZE, stride=2 * FOX_HEADS), :].astype(BF16)
        k = jnp.concatenate([rows(jj, 0) for jj in range(pp)], axis=0)
        v = jnp.concatenate([rows(jj, 1) for jj in range(pp)], axis=0)
        _online_update(_dot_nt(qs[h], k) - f_all[h:h + 1, :], v, m_ref.at[h], l_ref.at[h], acc_ref.at[h])

    @pl.when(i == n_steps - 1)
    def _():
        f_new = carry_ref[...] + lfn_ref[...]
        live = _new_row_live()
        for h in range(FOX_HEADS):
            s = _dot_nt(qs[h], kn_ref[h].astype(BF16)) - f_new[h:h + 1, :NEW_ROWS]
            _online_update(jnp.where(live, s, NEG_INF), vn_ref[h].astype(BF16),
                           m_ref.at[h], l_ref.at[h], acc_ref.at[h])
            o_ref[h] = acc_ref[h] / l_ref[h]


def fox_decode(page_table, cache_kv, lf_t, layer, q, k_new, v_new, lf_new):
    bsz, n_pages = page_table.shape
    pp = DEC_PAGES_PER_STEP
    n_steps = n_pages // pp
    hq = pl.BlockSpec((None, FOX_HEADS, QROWS, HEAD_DIM), lambda b, i, pt: (b, 0, 0, 0))
    hn = pl.BlockSpec((None, FOX_HEADS, NEW_ROWS, HEAD_DIM), lambda b, i, pt: (b, 0, 0, 0))
    st = lambda d: pltpu.VMEM((FOX_HEADS, QROWS, d), F32)
    grid_spec = pltpu.PrefetchScalarGridSpec(
        num_scalar_prefetch=1, grid=(bsz, n_steps),
        in_specs=(_page_specs(pp, (None, None, PAGE_SIZE * FOX_HEADS * 2, HEAD_DIM), layer, (0, 0))
                  + _page_specs(pp, (None, None, LF_ROWS, PAGE_SIZE), layer, (0, 0))
                  + [hq, hn, hn, pl.BlockSpec((None, LF_ROWS, LANES), lambda b, i, pt: (b, 0, 0))]),
        out_specs=hq,
        scratch_shapes=[st(1), st(1), st(HEAD_DIM), pltpu.VMEM((LF_ROWS, LANES), F32)])
    return pl.pallas_call(
        functools.partial(_fox_dec_kernel, pp=pp, n_steps=n_steps),
        grid_spec=grid_spec,
        out_shape=jax.ShapeDtypeStruct((bsz, FOX_HEADS, QROWS, HEAD_DIM), F32),
        compiler_params=_cparams(("parallel", "arbitrary")),
        name="fox_decode",
    )(page_table, *([cache_kv] * pp), *([lf_t] * pp), q, k_new, v_new, lf_new)


def _diff_dec_kernel(pt_ref, *refs, pp, n_steps, lam_init):
    kv_refs, b_refs = refs[:pp], refs[pp:2 * pp]
    q_ref, kn_ref, vn_ref, bn_ref, lam_ref, g_ref, o_ref, m_ref, l_ref, acc_ref = refs[2 * pp:]
    i = pl.program_id(1)

    @pl.when(i == 0)
    def _():
        for h in range(DIFF_HEADS):
            for c in range(2):
                _init_state(m_ref.at[h, c], l_ref.at[h, c], acc_ref.at[h, c])

    qs = [[(q_ref[h, c] * SCALE).astype(BF16) for c in range(2)] for h in range(DIFF_HEADS)]
    bias = jnp.concatenate([b_refs[jj][...] for jj in range(pp)], axis=1)
    for h in range(DIFF_HEADS):
        v = jnp.concatenate([kv_refs[jj][:, 1, h, :].astype(BF16) for jj in range(pp)], axis=0)
        for c in range(2):
            k = jnp.concatenate([kv_refs[jj][:, 0, h, c * HEAD_DIM:(c + 1) * HEAD_DIM].astype(BF16)
                                 for jj in range(pp)], axis=0)
            _online_update(_dot_nt(qs[h][c], k) + bias[h:h + 1, :], v,
                           m_ref.at[h, c], l_ref.at[h, c], acc_ref.at[h, c])

    @pl.when(i == n_steps - 1)
    def _():
        live = _new_row_live()
        lam = _diff_lambda(lam_ref, lam_init)
        for h in range(DIFF_HEADS):
            v = vn_ref[h].astype(BF16)
            for c in range(2):
                s = _dot_nt(qs[h][c], kn_ref[h, c].astype(BF16)) + bn_ref[h:h + 1, :NEW_ROWS]
                _online_update(jnp.where(live, s, NEG_INF), v, m_ref.at[h, c], l_ref.at[h, c], acc_ref.at[h, c])
            o = acc_ref[h, 0] / l_ref[h, 0] - lam * (acc_ref[h, 1] / l_ref[h, 1])
            o_ref[h] = _rms(o, g_ref[...]) * (1.0 - lam_init)


def diff_decode(page_table, cache_kv, bias_t, layer, q, k_new, v_new, bias_new, lam_p, subln_g, lam_init):
    bsz, n_pages = page_table.shape
    pp = DEC_PAGES_PER_STEP
    n_steps = n_pages // pp
    w = DIFF_VDIM

    def bias_spec(jj):
        return pl.BlockSpec((None, QROWS, PAGE_SIZE), lambda b, i, pt: (i * pp + jj, 0, 0))

    const = lambda shape: pl.BlockSpec(shape, lambda b, i, pt: (0,) * len(shape))
    st = lambda d: pltpu.VMEM((DIFF_HEADS, 2, QROWS, d), F32)
    out_spec = pl.BlockSpec((None, DIFF_HEADS, QROWS, w), lambda b, i, pt: (b, 0, 0, 0))
    grid_spec = pltpu.PrefetchScalarGridSpec(
        num_scalar_prefetch=1, grid=(bsz, n_steps),
        in_specs=(_page_specs(pp, (None, None, PAGE_SIZE, 2, DIFF_HEADS, w), layer, (0, 0, 0, 0))
                  + [bias_spec(jj) for jj in range(pp)]
                  + [pl.BlockSpec((None, DIFF_HEADS, 2, QROWS, HEAD_DIM), lambda b, i, pt: (b, 0, 0, 0, 0)),
                     pl.BlockSpec((None, DIFF_HEADS, 2, NEW_ROWS, HEAD_DIM), lambda b, i, pt: (b, 0, 0, 0, 0)),
                     pl.BlockSpec((None, DIFF_HEADS, NEW_ROWS, w), lambda b, i, pt: (b, 0, 0, 0)),
                     const((QROWS, LANES)), const((4, HEAD_DIM)), const((1, w))]),
        out_specs=out_spec,
        scratch_shapes=[st(1), st(1), st(w)])
    return pl.pallas_call(
        functools.partial(_diff_dec_kernel, pp=pp, n_steps=n_steps, lam_init=lam_init),
        grid_spec=grid_spec,
        out_shape=jax.ShapeDtypeStruct((bsz, DIFF_HEADS, QROWS, w), F32),
        compiler_params=_cparams(("parallel", "arbitrary")),
        name="diff_decode",
    )(page_table, *([cache_kv] * pp), *([bias_t] * pp), q, k_new, v_new, bias_new, lam_p, subln_g.reshape(1, w))


def _cmp_dec_kernel(pt_ref, *refs, pp):
    kv_refs = refs[:pp]
    pos_ref, w1_ref, w2_ref, o_ref, g_ref = refs[pp:]
    rows = pp * PAGE_SIZE
    for s in range(2):
        for k in range(NSA_KV_HEADS):
            for jj in range(pp):
                g_ref[s, pl.ds(k * rows + jj * PAGE_SIZE, PAGE_SIZE), :] = (
                    kv_refs[jj][pl.ds(k * 4 + s, PAGE_SIZE, stride=4 * NSA_KV_HEADS), :] + pos_ref[s])
    per_head = 2 * pp
    nrow = NSA_KV_HEADS * per_head
    for s in range(2):
        acc = jnp.zeros((nrow, HEAD_DIM), F32)
        for l in range(CMP_BLOCK):
            xl = g_ref.at[s][pl.ds(l, nrow, stride=CMP_BLOCK), :]
            acc = acc + jnp.dot(xl.astype(BF16), w1_ref[s, l], preferred_element_type=F32)
        out = jnp.dot(_gelu_tanh(acc).astype(BF16), w2_ref[s], preferred_element_type=F32)
        for k in range(NSA_KV_HEADS):
            o_ref[s, k] = out[k * per_head:(k + 1) * per_head, :]


def compress_decode(page_table, cache_kv, layer, pos2, w1, w2):
    bsz, n_pages = page_table.shape
    pp = min(CMP_PAGES_PER_STEP, n_pages)
    per_page = PAGE_SIZE // CMP_BLOCK
    const = lambda shape: pl.BlockSpec(shape, lambda b, i, pt: (0,) * len(shape))
    grid_spec = pltpu.PrefetchScalarGridSpec(
        num_scalar_prefetch=1, grid=(bsz, n_pages // pp),
        in_specs=(_page_specs(pp, (None, None, PAGE_SIZE * NSA_KV_HEADS * 4, HEAD_DIM), layer, (0, 0))
                  + [const((2, PAGE_SIZE, HEAD_DIM)), const((2, CMP_BLOCK, HEAD_DIM, HEAD_DIM)),
                     const((2, HEAD_DIM, HEAD_DIM))]),
        out_specs=pl.BlockSpec((None, 2, NSA_KV_HEADS, per_page * pp, HEAD_DIM), lambda b, i, pt: (b, 0, 0, i, 0)),
        scratch_shapes=[pltpu.VMEM((2, NSA_KV_HEADS * pp * PAGE_SIZE, HEAD_DIM), F32)])
    return pl.pallas_call(
        functools.partial(_cmp_dec_kernel, pp=pp),
        grid_spec=grid_spec,
        out_shape=jax.ShapeDtypeStruct((bsz, 2, NSA_KV_HEADS, per_page * n_pages, HEAD_DIM), F32),
        compiler_params=_cparams(("parallel", "arbitrary")),
        name="nsa_compress_decode",
    )(page_table, *([cache_kv] * pp), pos2, w1, w2)


def _cmp_sel_dec_kernel(q_ref, kc_ref, vc_ref, o_ref, idx_ref, *, n_blk, n_pick):
    q = (q_ref[...] * SCALE).astype(BF16)
    lc = _dot_nt(q, kc_ref[...].astype(BF16))
    e = jnp.exp(lc - jnp.max(lc, axis=-1, keepdims=True))
    pc = e / jnp.sum(e, axis=-1, keepdims=True)
    o_ref[...] = jnp.dot(pc.astype(BF16), vc_ref[...].astype(BF16), preferred_element_type=F32)
    rowi = lax.broadcasted_iota(jnp.int32, (QROWS, n_blk), 0)
    blk = lax.broadcasted_iota(jnp.int32, (QROWS, n_blk), 1)
    score = jnp.broadcast_to(jnp.sum(jnp.where(rowi < NSA_GROUP, pc, 0.0), axis=0, keepdims=True), (QROWS, n_blk))
    score = jnp.where((blk == 0) | (blk == n_blk - 1), FORCE_SCORE, score)
    rank = jnp.zeros((QROWS, n_blk), jnp.int32)
    for j in range(n_blk):
        cj = score[:, j:j + 1]
        rank = rank + jnp.where((cj > score) | ((cj == score) & (blk > j)), 1, 0)
    lane = lax.broadcasted_iota(jnp.int32, (QROWS, LANES), 1)
    blk_f = blk.astype(F32)
    picked = jnp.zeros((QROWS, LANES), F32)
    for r in range(n_pick):
        v = jnp.sum(jnp.where(rank == r, blk_f, 0.0), axis=-1, keepdims=True)
        picked = jnp.where(lane == r, v, picked)
    idx_ref[...] = picked.astype(jnp.int32)


def cmp_select_decode(q8, cmp_kv):
    bsz = q8.shape[0]
    n_blk = cmp_kv.shape[3]
    n_pick = N_SELECT - 1
    assert n_blk >= n_pick
    hspec = pl.BlockSpec((None, None, QROWS, HEAD_DIM), lambda b, k: (b, k, 0, 0))
    return pl.pallas_call(
        functools.partial(_cmp_sel_dec_kernel, n_blk=n_blk, n_pick=n_pick),
        grid=(bsz, NSA_KV_HEADS),
        in_specs=[hspec,
                  pl.BlockSpec((None, None, None, n_blk, HEAD_DIM), lambda b, k: (b, 0, k, 0, 0)),
                  pl.BlockSpec((None, None, None, n_blk, HEAD_DIM), lambda b, k: (b, 1, k, 0, 0))],
        out_specs=[hspec, pl.BlockSpec((None, None, QROWS, LANES), lambda b, k: (b, k, 0, 0))],
        out_shape=[jax.ShapeDtypeStruct((bsz, NSA_KV_HEADS, QROWS, HEAD_DIM), F32),
                   jax.ShapeDtypeStruct((bsz, NSA_KV_HEADS, QROWS, LANES), jnp.int32)],
        compiler_params=_cparams(("parallel", "parallel")),
        name="nsa_cmp_select_decode",
    )(q8, cmp_kv, cmp_kv)


def _slc_dec_kernel(pt_ref, sel_ref, *refs, n_pick):
    kv_refs, b_refs = refs[:NSA_KV_HEADS], refs[NSA_KV_HEADS:2 * NSA_KV_HEADS]
    q_ref, kn_ref, vn_ref, bn_ref, o_ref, m_ref, l_ref, acc_ref = refs[2 * NSA_KV_HEADS:]
    i = pl.program_id(1)

    @pl.when(i == 0)
    def _():
        for k in range(NSA_KV_HEADS):
            _init_state(m_ref.at[k], l_ref.at[k], acc_ref.at[k])

    qs = [(q_ref[k] * SCALE).astype(BF16) for k in range(NSA_KV_HEADS)]
    for k in range(NSA_KV_HEADS):
        rows = lambda s: kv_refs[k][pl.ds(k * 4 + s, CMP_BLOCK, stride=4 * NSA_KV_HEADS), :].astype(BF16)
        ks, vs = rows(2), rows(3)
        _online_update(_dot_nt(qs[k], ks) + b_refs[k][...], vs, m_ref.at[k], l_ref.at[k], acc_ref.at[k])

    @pl.when(i == n_pick - 1)
    def _():
        live = _new_row_live()
        for k in range(NSA_KV_HEADS):
            s = _dot_nt(qs[k], kn_ref[k].astype(BF16)) + bn_ref[k][:, :NEW_ROWS]
            _online_update(jnp.where(live, s, NEG_INF), vn_ref[k].astype(BF16), m_ref.at[k], l_ref.at[k], acc_ref.at[k])
            o_ref[k] = acc_ref[k] / l_ref[k]


def slc_decode(page_table, sel, cache_kv, layer, q8, k_new, v_new, bias_blk, bias_new):
    bsz = q8.shape[0]
    n_pick = N_SELECT - 1
    per_page = PAGE_SIZE // CMP_BLOCK

    def kv_spec(k):
        def idx(b, i, pt, sel):
            blk = sel[b, k * N_SELECT + i]
            return (pt[b, blk // per_page], layer, blk % per_page, 0)
        return pl.BlockSpec((None, None, CMP_BLOCK * NSA_KV_HEADS * 4, HEAD_DIM), idx)

    def bias_spec(k):
        return pl.BlockSpec((None, None, QROWS, CMP_BLOCK), lambda b, i, pt, sel: (k, sel[b, k * N_SELECT + i], 0, 0))

    hq = pl.BlockSpec((None, NSA_KV_HEADS, QROWS, HEAD_DIM), lambda b, i, pt, sel: (b, 0, 0, 0))
    hn = pl.BlockSpec((None, NSA_KV_HEADS, NEW_ROWS, HEAD_DIM), lambda b, i, pt, sel: (b, 0, 0, 0))
    st = lambda d: pltpu.VMEM((NSA_KV_HEADS, QROWS, d), F32)
    grid_spec = pltpu.PrefetchScalarGridSpec(
        num_scalar_prefetch=2, grid=(bsz, n_pick),
        in_specs=([kv_spec(k) for k in range(NSA_KV_HEADS)] + [bias_spec(k) for k in range(NSA_KV_HEADS)]
                  + [hq, hn, hn, pl.BlockSpec((NSA_KV_HEADS, QROWS, LANES), lambda b, i, pt, sel: (0, 0, 0))]),
        out_specs=hq,
        scratch_shapes=[st(1), st(1), st(HEAD_DIM)])
    return pl.pallas_call(
        functools.partial(_slc_dec_kernel, n_pick=n_pick),
        grid_spec=grid_spec,
        out_shape=jax.ShapeDtypeStruct((bsz, NSA_KV_HEADS, QROWS, HEAD_DIM), F32),
        compiler_params=_cparams(("parallel", "arbitrary")),
        name="nsa_slc_decode",
    )(page_table, sel, *([cache_kv] * NSA_KV_HEADS), *([bias_blk] * NSA_KV_HEADS), q8, k_new, v_new, bias_new)


def _win_dec_kernel(q_ref, st_ref, kn_ref, vn_ref, wb_ref, bn_ref, oc_ref, os_ref, g_ref, o_ref):
    live = _new_row_live()
    for k in range(NSA_KV_HEADS):
        q = (q_ref[k] * SCALE).astype(BF16)
        s = _dot_nt(q, st_ref[:, 0, k, :].astype(BF16)) + wb_ref[k]
        sn = jnp.where(live, _dot_nt(q, kn_ref[k].astype(BF16)) + bn_ref[k][:, :NEW_ROWS], NEG_INF)
        m = jnp.maximum(jnp.max(s, axis=-1, keepdims=True), jnp.max(sn, axis=-1, keepdims=True))
        p = jnp.exp(s - m)
        pn = jnp.exp(sn - m)
        den = jnp.sum(p, axis=-1, keepdims=True) + jnp.sum(pn, axis=-1, keepdims=True)
        ow = (jnp.dot(p.astype(BF16), st_ref[:, 1, k, :].astype(BF16), preferred_element_type=F32)
              + jnp.dot(pn.astype(BF16), vn_ref[k].astype(BF16), preferred_element_type=F32)) / den
        o_ref[k] = g_ref[k, 0] * oc_ref[k] + g_ref[k, 1] * os_ref[k] + g_ref[k, 2] * ow


def win_combine_decode(q8, state, layer, k_new, v_new, bias_win, bias_new, o_cmp, o_slc, gates):
    bsz = q8.shape[0]
    wlen = state.shape[2]
    hq = pl.BlockSpec((None, NSA_KV_HEADS, QROWS, HEAD_DIM), lambda b: (b, 0, 0, 0))
    hn = pl.BlockSpec((None, NSA_KV_HEADS, NEW_ROWS, HEAD_DIM), lambda b: (b, 0, 0, 0))
    return pl.pallas_call(
        _win_dec_kernel,
        grid=(bsz,),
        in_specs=[hq,
                  pl.BlockSpec((None, None, wlen, 2, NSA_KV_HEADS, HEAD_DIM), lambda b: (b, layer, 0, 0, 0, 0)),
                  hn, hn,
                  pl.BlockSpec((NSA_KV_HEADS, QROWS, wlen), lambda b: (0, 0, 0)),
                  pl.BlockSpec((NSA_KV_HEADS, QROWS, LANES), lambda b: (0, 0, 0)),
                  hq, hq,
                  pl.BlockSpec((None, NSA_KV_HEADS, 3, QROWS, HEAD_DIM), lambda b: (b, 0, 0, 0, 0))],
        out_specs=hq,
        out_shape=jax.ShapeDtypeStruct((bsz, NSA_KV_HEADS, QROWS, HEAD_DIM), F32),
        compiler_params=_cparams(("parallel",)),
        name="nsa_win_decode",
    )(q8, state, k_new, v_new, bias_win, bias_new, o_cmp, o_slc, gates)


def _decode_tables(rel_bias, past_len, wlen):
    dist = past_len - jnp.arange(past_len + 1, dtype=jnp.int32)
    nf = jnp.maximum(dist, 1).astype(F32)
    large = MAX_EXACT + (jnp.log(nf / MAX_EXACT) / math.log(MAX_DISTANCE / MAX_EXACT)
                         * (N_BUCKETS - MAX_EXACT)).astype(jnp.int32)
    bucket = jnp.where(dist < MAX_EXACT, dist, jnp.minimum(large, N_BUCKETS - 1))
    bias_all = rel_bias[bucket].astype(F32)
    nsa, dif = bias_all[:, :NSA_HEADS], bias_all[:, NSA_HEADS:]
    n_pages, n_blk = past_len // PAGE_SIZE, past_len // CMP_BLOCK
    pad_rows = lambda x, axis: jnp.pad(x, [(0, QROWS - x.shape[axis]) if a == axis else (0, 0) for a in range(x.ndim)])
    lanes = lambda x: jnp.broadcast_to(x[..., None], x.shape + (LANES,))
    diff_t = pad_rows(jnp.transpose(dif[:past_len].reshape(n_pages, PAGE_SIZE, DIFF_HEADS), (0, 2, 1)), 1)
    diff_new = pad_rows(lanes(dif[past_len]), 0)
    nsa_blk = pad_rows(jnp.transpose(nsa[:past_len].reshape(n_blk, CMP_BLOCK, NSA_KV_HEADS, NSA_GROUP), (2, 0, 3, 1)), 2)
    nsa_new = pad_rows(lanes(nsa[past_len].reshape(NSA_KV_HEADS, NSA_GROUP)), 1)
    wdist = dist[past_len - wlen:past_len]
    win = jnp.where((wdist < WINDOW)[None, None, :],
                    jnp.transpose(nsa[past_len - wlen:past_len].reshape(wlen, NSA_KV_HEADS, NSA_GROUP), (1, 2, 0)),
                    NEG_INF)
    return dict(diff_t=diff_t, diff_new=diff_new, nsa_blk=nsa_blk, nsa_new=nsa_new, nsa_win=pad_rows(win, 1))


def _sample_attention(a, s_act, layer, caches, tables, w, nsa_phi_pos, diff_lambda, subln_g, lam_init):
    page_table, cache_fox_kv, lf_t, cache_nsa_kv, cache_diff_kv, state_nsa_win = caches
    bsz = a.shape[0]
    cols = lambda c0, c1: a[:, c0 * LANES:c1 * LANES]
    rep = lambda x: jnp.broadcast_to(x[..., None, :], x.shape[:-1] + (QROWS, x.shape[-1]))
    row0 = lambda x: jnp.pad(x[..., None, :], [(0, 0)] * (x.ndim - 1) + [(0, NEW_ROWS - 1), (0, 0)])
    lanes = lambda x: jnp.broadcast_to(x[..., None], x.shape + (LANES,))

    heads = lambda c0: cols(c0, c0 + FOX_HEADS).reshape(bsz, FOX_HEADS, HEAD_DIM)
    lf_new = lanes(jnp.pad(s_act[:, :N_FF], ((0, 0), (0, LF_ROWS - N_FF))))
    o_fox = fox_decode(page_table, cache_fox_kv, lf_t, layer, rep(heads(COL_FQ)), row0(heads(COL_FK)),
                       row0(heads(COL_FV)), lf_new)[:, :, 0]

    dq = cols(COL_DQ, COL_DK).reshape(bsz, DIFF_HEADS, 2, HEAD_DIM)
    dk = cols(COL_DK, COL_DV).reshape(bsz, DIFF_HEADS, 2, HEAD_DIM)
    dv = cols(COL_DV, COL_DV + 8).reshape(bsz, DIFF_HEADS, DIFF_VDIM)
    o_diff = diff_decode(page_table, cache_diff_kv, tables["diff_t"], layer, rep(dq), row0(dk), row0(dv),
                         tables["diff_new"], diff_lambda, subln_g, lam_init)[:, :, 0]

    nq = cols(COL_NQ, COL_NKV).reshape(bsz, NSA_KV_HEADS, NSA_GROUP, HEAD_DIM)
    q8 = jnp.pad(nq, ((0, 0), (0, 0), (0, QROWS - NSA_GROUP), (0, 0)))
    nkv = cols(COL_NKV, COL_DQ).reshape(bsz, 6, NSA_KV_HEADS, HEAD_DIM)
    pos2 = jnp.tile(nsa_phi_pos, (1, PAGE_SIZE // CMP_BLOCK, 1))
    cmp_kv = compress_decode(page_table, cache_nsa_kv, layer, pos2, w["phi_w1"], w["phi_w2"])
    o_cmp, picked = cmp_select_decode(q8, cmp_kv)
    sel = jnp.pad(picked[:, :, 0, :N_SELECT - 1], ((0, 0), (0, 0), (0, 1))).reshape(bsz, NSA_KV_HEADS * N_SELECT)
    o_slc = slc_decode(page_table, sel, cache_nsa_kv, layer, q8, row0(nkv[:, 2]), row0(nkv[:, 3]),
                       tables["nsa_blk"], tables["nsa_new"])
    ng = s_act[:, N_FF:N_FF + N_NG].reshape(bsz, NSA_KV_HEADS, NSA_GROUP, 3)
    gates = lanes(jnp.pad(jnp.transpose(ng, (0, 1, 3, 2)), ((0, 0), (0, 0), (0, 0), (0, QROWS - NSA_GROUP))))
    o_nsa = win_combine_decode(q8, state_nsa_win, layer, row0(nkv[:, 4]), row0(nkv[:, 5]), tables["nsa_win"],
                               tables["nsa_new"], o_cmp, o_slc, gates)[:, :, :NSA_GROUP]
    win_new = nkv[:, 4:6].reshape(bsz, 1, 2, NSA_KV_HEADS, HEAD_DIM)
    win_state = jnp.concatenate([state_nsa_win[:, layer, 1:], win_new], axis=1)
    return (o_fox.reshape(bsz, FW).astype(BF16), o_nsa.reshape(bsz, FW).astype(BF16),
            o_diff.reshape(bsz, DW).astype(BF16), win_state)


def _prep_layer_weights(l, w_in, fox_fbias, nsa_phi_w1, nsa_phi_w2, w_br_fox, w_br_nsa, w_br_diff, w_o,
                        ffn_w_gate, ffn_w_up, ffn_w_down):
    wl = w_in[l]
    c0, c1 = 3 * FW, 3 * FW + N_FF
    c2 = c1 + FW + NKVW
    c3 = c2 + N_NG
    c4 = c3 + 3 * DW
    w_pack = jnp.concatenate([wl[:, :c0], wl[:, c1:c2], wl[:, c3:c4]], axis=1).astype(BF16)
    w_small = jnp.concatenate([wl[:, c0:c1], wl[:, c2:c3]], axis=1)
    w_small = jnp.pad(w_small, ((0, 0), (0, LANES - w_small.shape[1]))).astype(BF16)
    b_small = jnp.pad(fox_fbias[l], (0, LANES - N_FF)).reshape(1, LANES).astype(F32)
    return dict(
        w_pack=w_pack, w_small=w_small, b_small=b_small, w_bg=wl[:, c4:].astype(BF16),
        phi_w1=nsa_phi_w1[l].astype(BF16), phi_w2=nsa_phi_w2[l].astype(BF16),
        w_br_fox=w_br_fox[l].astype(BF16), w_br_nsa=w_br_nsa[l].astype(BF16),
        w_br_diff=w_br_diff[l].astype(BF16), w_o=w_o[l].astype(BF16),
        layer=l, wg=ffn_w_gate, wu=ffn_w_up, wd=ffn_w_down,
    )


def _ffn(xn, w, i):
    lead = (w["layer"], i)
    act = gateup(xn, w["wg"], w["wu"], lead)
    return matmul(act, w["wd"], tm=512, tn=512, lead=lead, name="ffn_down")


def _layer(x, xn, w, g, g_next, attend):
    h, u = resid_norm(x, _ffn(xn, w, 0), g[1], g[2], 0.5)
    a = matmul(u, w["w_pack"], tm=1024, tn=768, name="proj_pack")
    s_act = small_proj(u, w["w_small"], w["b_small"])
    gates = matmul(u, w["w_bg"], tm=1024, tn=1024, act="sigmoid", name="proj_gates")
    o_fox, o_nsa, o_diff, extra = attend(a, s_act)
    merged = merge_branches(o_fox, o_nsa, o_diff, w["w_br_fox"], w["w_br_nsa"], w["w_br_diff"], gates)
    z = matmul(merged, w["w_o"], tm=1024, tn=1024, name="proj_out")
    h, u3 = resid_norm(h, z, g[3], g[4], 1.0)
    y, yn = resid_norm(h, _ffn(u3, w, 1), g[5], g_next, 0.5)
    return y, yn, a, s_act, extra


def kernel(x_prompt, x_sample, cache_fox_kv, cache_fox_logf, cache_nsa_kv, cache_diff_kv, state_nsa_win,
           page_table, norm_g, w_in, fox_fbias, nsa_phi_w1, nsa_phi_w2, nsa_phi_pos, diff_lambda,
           diff_subln_g, w_br_fox, w_br_nsa, w_br_diff, w_o, ffn_w_gate, ffn_w_up, ffn_w_down, rel_bias):
    bp, t, d = x_prompt.shape
    bs = x_sample.shape[0]
    ms = 16
    hp = x_prompt.reshape(bp * t, d)
    hs = jnp.pad(x_sample.reshape(bs, d), ((0, ms - bs), (0, 0)))
    rb = _rev_bias(rel_bias, t)
    rb_nsa, rb_diff = rb[:NSA_HEADS], rb[NSA_HEADS:]

    tables = _decode_tables(rel_bias, page_table.shape[1] * PAGE_SIZE, state_nsa_win.shape[2])
    lf_t = jnp.pad(jnp.swapaxes(cache_fox_logf, 2, 3), ((0, 0), (0, 0), (0, LF_ROWS - FOX_HEADS), (0, 0)))
    flat = lambda pool: jnp.swapaxes(pool, 3, 4).reshape(pool.shape[:2] + (-1, HEAD_DIM))
    caches = (page_table, flat(cache_fox_kv), lf_t, flat(cache_nsa_kv), cache_diff_kv, state_nsa_win)

    ffn_w_gate, ffn_w_up, ffn_w_down = (x.astype(BF16) for x in (ffn_w_gate, ffn_w_up, ffn_w_down))

    hpn = rms_cast(hp, norm_g[0, 0])
    hsn = rms_cast(hs, norm_g[0, 0])
    st_p, st_s = [], []
    for l in range(DEPTH):
        w = _prep_layer_weights(l, w_in, fox_fbias, nsa_phi_w1, nsa_phi_w2, w_br_fox, w_br_nsa, w_br_diff,
                                w_o, ffn_w_gate, ffn_w_up, ffn_w_down)
        g = norm_g[l]
        g_next = norm_g[l + 1, 0] if l + 1 < DEPTH else None
        lam_init = 0.8 - 0.6 * math.exp(-0.3 * l)

        def attend_prompt(a, s_act):
            a3 = a.reshape(bp, t, PACK_W)
            s3 = s_act.reshape(bp, t, LANES)
            f_all = cumsum_time(s3)
            f_rows = jnp.transpose(f_all[:, :, :N_FF], (0, 2, 1)).reshape(bp, FOX_HEADS, 1, t)
            o_fox = fox_prompt(a3, f_rows)
            o_diff = diff_prompt(a3, rb_diff, diff_lambda[l], diff_subln_g[l], lam_init)
            cmp_kv = compress_prompt(a3, w["phi_w1"], w["phi_w2"], nsa_phi_pos[l])
            o_cmp, sel = cmp_select_prompt(a3, cmp_kv)
            o_nsa = slc_win_prompt(a3, sel, rb_nsa, o_cmp, s3)
            return (o_fox.reshape(bp * t, FW), o_nsa.reshape(bp * t, FW), o_diff.reshape(bp * t, DW), None)

        def attend_sample(a, s_act):
            o_fox, o_nsa, o_diff, win_state = _sample_attention(
                a[:bs], s_act[:bs], l, caches, tables, w, nsa_phi_pos[l], diff_lambda[l], diff_subln_g[l], lam_init)
            padr = lambda o: jnp.pad(o, ((0, ms - bs), (0, 0)))
            return padr(o_fox), padr(o_nsa), padr(o_diff), win_state

        hp, hpn, a_p, s_p, _ = _layer(hp, hpn, w, g, g_next, attend_prompt)
        hs, hsn, a_s, s_s, win_s = _layer(hs, hsn, w, g, g_next, attend_sample)
        st_p.append((a_p.reshape(bp, t, PACK_W), s_p.reshape(bp, t, LANES)))
        st_s.append((a_s[:bs].reshape(bs, 1, PACK_W), s_s[:bs].reshape(bs, 1, LANES), win_s))

    def states(st, nb_, tt):
        stack = lambda f: jnp.stack([f(s) for s in st], axis=1)
        fox_kv = stack(lambda s: s[0][:, :, FW:3 * FW].reshape(nb_, tt, 2, FOX_HEADS, HEAD_DIM))
        logf = stack(lambda s: s[1][:, :, :N_FF])
        nsa_kv = stack(lambda s: s[0][:, :, COL_NKV * LANES:(COL_NKV + 12) * LANES].reshape(
            nb_, tt, 4, NSA_KV_HEADS, HEAD_DIM))
        diff_kv = stack(lambda s: s[0][:, :, COL_DK * LANES:].reshape(nb_, tt, 2, DIFF_HEADS, DIFF_VDIM))
        return fox_kv, logf, nsa_kv, diff_kv

    fox_kv_p, logf_p, nsa_kv_p, diff_kv_p = states(st_p, bp, t)
    fox_kv_s, logf_s, nsa_kv_s, diff_kv_s = states(st_s, bs, 1)
    win_lo = (COL_NKV + 12) * LANES
    nsa_win_p = jnp.stack([s[0][:, t - min(WINDOW, t):, win_lo:COL_DQ * LANES].reshape(
        bp, min(WINDOW, t), 2, NSA_KV_HEADS, HEAD_DIM) for s in st_p], axis=1)
    nsa_win_s = jnp.stack([s[2] for s in st_s], axis=1)
    return (hp.reshape(bp, t, d), hs[:bs].reshape(bs, 1, d), fox_kv_p, fox_kv_s, logf_p, logf_s,
            nsa_kv_p, nsa_kv_s, diff_kv_p, diff_kv_s, nsa_win_p, nsa_win_s)
```

```python
import functools
import math

import jax
import jax.numpy as jnp
from jax import lax
from jax.experimental import pallas as pl
from jax.experimental.pallas import tpu as pltpu

F32 = jnp.float32
BF16 = jnp.bfloat16

D_MODEL = 4096
DEPTH = 2
PAST_LEN = 16384
PAGE_SIZE = 128
HEAD_DIM = 128
FOX_HEADS = 12
NSA_HEADS = 12
NSA_KV_HEADS = 3
NSA_GROUP = NSA_HEADS // NSA_KV_HEADS
DIFF_HEADS = 4
DIFF_VDIM = 2 * HEAD_DIM
CMP_BLOCK = 64
N_SELECT = 16
WINDOW = 512
N_BUCKETS = 32
MAX_EXACT = N_BUCKETS // 2
MAX_DISTANCE = 1024
D_FF = 11008
FORCE_SCORE = 1.0e4
NEG_INF = -1.0e30
EPS = 1.0e-6
SCALE = HEAD_DIM ** -0.5

VMEM_LIMIT_BYTES = 56 * 1024 * 1024
LANES = 128

FW = FOX_HEADS * HEAD_DIM
NKVW = 6 * NSA_KV_HEADS * HEAD_DIM
DW = DIFF_HEADS * DIFF_VDIM
COL_FQ, COL_FK, COL_FV = 0, 12, 24
COL_NQ, COL_NKV = 36, 48
COL_DQ, COL_DK, COL_DV = 66, 74, 82
PACK_W = 90 * LANES
N_FF = FOX_HEADS
N_NG = 3 * NSA_HEADS

FOX_TQ = 1024
DIFF_TQ = 512
NSA_TQ = 512
RB_LEN = 4096


def _cparams(sem):
    return pltpu.CompilerParams(dimension_semantics=sem, vmem_limit_bytes=VMEM_LIMIT_BYTES)


def _rms(x, g):
    return x * lax.rsqrt(jnp.mean(x * x, axis=-1, keepdims=True) + EPS) * g


def _rms_kernel(x_ref, g_ref, o_ref):
    o_ref[...] = _rms(x_ref[...], g_ref[...]).astype(o_ref.dtype)


def rms_cast(x, g, *, tr=256):
    m, d = x.shape
    tr = min(tr, m)
    return pl.pallas_call(
        _rms_kernel,
        grid=(m // tr,),
        in_specs=[pl.BlockSpec((tr, d), lambda i: (i, 0)),
                  pl.BlockSpec((1, d), lambda i: (0, 0))],
        out_specs=pl.BlockSpec((tr, d), lambda i: (i, 0)),
        out_shape=jax.ShapeDtypeStruct((m, d), BF16),
        compiler_params=_cparams(("parallel",)),
        name="rms_cast",
    )(x, g.reshape(1, d))


def _resid_norm_kernel(x_ref, y_ref, gp_ref, gn_ref, h_ref, u_ref, *, coef):
    h = x_ref[...] + coef * _rms(y_ref[...], gp_ref[...])
    h_ref[...] = h
    u_ref[...] = _rms(h, gn_ref[...]).astype(u_ref.dtype)


def _resid_kernel(x_ref, y_ref, gp_ref, h_ref, *, coef):
    h_ref[...] = x_ref[...] + coef * _rms(y_ref[...], gp_ref[...])


def resid_norm(x, y, g_post, g_next, coef, *, tr=256):
    m, d = x.shape
    tr = min(tr, m)
    row = pl.BlockSpec((tr, d), lambda i: (i, 0))
    vec = pl.BlockSpec((1, d), lambda i: (0, 0))
    if g_next is None:
        return pl.pallas_call(
            functools.partial(_resid_kernel, coef=coef),
            grid=(m // tr,),
            in_specs=[row, row, vec],
            out_specs=row,
            out_shape=jax.ShapeDtypeStruct((m, d), F32),
            compiler_params=_cparams(("parallel",)),
            name="resid",
        )(x, y, g_post.reshape(1, d)), None
    return pl.pallas_call(
        functools.partial(_resid_norm_kernel, coef=coef),
        grid=(m // tr,),
        in_specs=[row, row, vec, vec],
        out_specs=[row, row],
        out_shape=[jax.ShapeDtypeStruct((m, d), F32), jax.ShapeDtypeStruct((m, d), BF16)],
        compiler_params=_cparams(("parallel",)),
        name="resid_norm",
    )(x, y, g_post.reshape(1, d), g_next.reshape(1, d))


def _sigmoid(x):
    return 1.0 / (1.0 + jnp.exp(-x))


def _mm_kernel(x_ref, w_ref, o_ref, *, act):
    acc = jnp.dot(x_ref[...], w_ref[...], preferred_element_type=F32)
    if act == "sigmoid":
        acc = _sigmoid(acc)
    o_ref[...] = acc.astype(o_ref.dtype)


def _weight_spec(w, lead, tn):
    k = w.shape[-2]
    return pl.BlockSpec((None,) * len(lead) + (k, tn), lambda i, j: tuple(lead) + (0, j))


def matmul(x, w, *, tm, tn, out_dtype=F32, act=None, lead=(), name="matmul"):
    m, k = x.shape
    n = w.shape[-1]
    tm = min(tm, m)
    return pl.pallas_call(
        functools.partial(_mm_kernel, act=act),
        grid=(m // tm, n // tn),
        in_specs=[pl.BlockSpec((tm, k), lambda i, j: (i, 0)),
                  _weight_spec(w, lead, tn)],
        out_specs=pl.BlockSpec((tm, tn), lambda i, j: (i, j)),
        out_shape=jax.ShapeDtypeStruct((m, n), out_dtype),
        compiler_params=_cparams(("parallel", "parallel")),
        name=name,
    )(x, w)


def _gateup_kernel(x_ref, wg_ref, wu_ref, o_ref):
    x = x_ref[...]
    g = jnp.dot(x, wg_ref[...], preferred_element_type=F32)
    u = jnp.dot(x, wu_ref[...], preferred_element_type=F32)
    o_ref[...] = (g * _sigmoid(g) * u).astype(o_ref.dtype)


def gateup(x, wg, wu, lead, *, tm=1024, tn=256):
    m, k = x.shape
    n = wg.shape[-1]
    tm = min(tm, m)
    wspec = _weight_spec(wg, lead, tn)
    return pl.pallas_call(
        _gateup_kernel,
        grid=(m // tm, n // tn),
        in_specs=[pl.BlockSpec((tm, k), lambda i, j: (i, 0)), wspec, wspec],
        out_specs=pl.BlockSpec((tm, tn), lambda i, j: (i, j)),
        out_shape=jax.ShapeDtypeStruct((m, n), BF16),
        compiler_params=_cparams(("parallel", "parallel")),
        name="ffn_gateup",
    )(x, wg, wu)


def _small_proj_kernel(x_ref, w_ref, b_ref, o_ref):
    s = jnp.dot(x_ref[...], w_ref[...], preferred_element_type=F32)
    lane = lax.broadcasted_iota(jnp.int32, s.shape, 1)
    z = s + b_ref[...]
    logsig = jnp.minimum(z, 0.0) - jnp.log1p(jnp.exp(-jnp.abs(z)))
    o_ref[...] = jnp.where(lane < N_FF, logsig, _sigmoid(s))


def small_proj(x, w, b, *, tm=1024):
    m, k = x.shape
    tm = min(tm, m)
    return pl.pallas_call(
        _small_proj_kernel,
        grid=(m // tm,),
        in_specs=[pl.BlockSpec((tm, k), lambda i: (i, 0)),
                  pl.BlockSpec((k, LANES), lambda i: (0, 0)),
                  pl.BlockSpec((1, LANES), lambda i: (0, 0))],
        out_specs=pl.BlockSpec((tm, LANES), lambda i: (i, 0)),
        out_shape=jax.ShapeDtypeStruct((m, LANES), F32),
        compiler_params=_cparams(("parallel",)),
        name="small_proj",
    )(x, w, b)


def _merge_kernel(of_ref, on_ref, od_ref, wf_ref, wn_ref, wd_ref, g0_ref, g1_ref, g2_ref, o_ref):
    a = jnp.dot(of_ref[...], wf_ref[...], preferred_element_type=F32)
    b = jnp.dot(on_ref[...], wn_ref[...], preferred_element_type=F32)
    c = jnp.dot(od_ref[...], wd_ref[...], preferred_element_type=F32)
    o_ref[...] = (g0_ref[...] * a + g1_ref[...] * b + g2_ref[...] * c).astype(o_ref.dtype)


def merge_branches(o_fox, o_nsa, o_diff, wf, wn, wd, gates, *, tm=1024, tn=512):
    m = o_fox.shape[0]
    tm = min(tm, m)
    nb = D_MODEL // tn
    xs = lambda kdim: pl.BlockSpec((tm, kdim), lambda i, j: (i, 0))
    ws = lambda kdim: pl.BlockSpec((kdim, tn), lambda i, j: (0, j))
    gs = lambda br: pl.BlockSpec((tm, tn), lambda i, j: (i, br * nb + j))
    return pl.pallas_call(
        _merge_kernel,
        grid=(m // tm, nb),
        in_specs=[xs(FW), xs(FW), xs(DW), ws(FW), ws(FW), ws(DW), gs(0), gs(1), gs(2)],
        out_specs=pl.BlockSpec((tm, tn), lambda i, j: (i, j)),
        out_shape=jax.ShapeDtypeStruct((m, D_MODEL), BF16),
        compiler_params=_cparams(("parallel", "parallel")),
        name="merge_branches",
    )(o_fox, o_nsa, o_diff, wf, wn, wd, gates, gates, gates)


def _dot_nt(a, b):
    return lax.dot_general(a, b, (((1,), (1,)), ((), ())), preferred_element_type=F32)


def _online_update(s, v, m_ref, l_ref, acc_ref):
    m_prev = m_ref[...]
    m_new = jnp.maximum(m_prev, jnp.max(s, axis=-1, keepdims=True))
    p = jnp.exp(s - m_new)
    alpha = jnp.exp(m_prev - m_new)
    l_ref[...] = alpha * l_ref[...] + jnp.sum(p, axis=-1, keepdims=True)
    acc_ref[...] = alpha * acc_ref[...] + jnp.dot(p.astype(BF16), v, preferred_element_type=F32)
    m_ref[...] = m_new


def _init_state(m_ref, l_ref, acc_ref):
    m_ref[...] = jnp.full(m_ref.shape, NEG_INF, F32)
    l_ref[...] = jnp.zeros(l_ref.shape, F32)
    acc_ref[...] = jnp.zeros(acc_ref.shape, F32)


def _causal_tile(tq):
    row = lax.broadcasted_iota(jnp.int32, (tq, tq), 0)
    col = lax.broadcasted_iota(jnp.int32, (tq, tq), 1)
    return col <= row


def _n_bias_tiles(t, tq):
    return min(t // tq, -(-(MAX_DISTANCE + tq - 1) // tq))


def _build_bias_tiles(rb_row, tiles_ref, n_tiles, tq, t, *, lead=()):
    for j in range(n_tiles):
        base = t - j * tq - tq
        win = rb_row(base, base + 2 * tq)
        x = jnp.broadcast_to(win, (tq, 2 * tq))
        y = pltpu.roll(x, tq + 1, 1, stride=1, stride_axis=0)[:, :tq]
        if j == 0:
            y = jnp.where(_causal_tile(tq), y, NEG_INF)
        tiles_ref[lead + (j,)] = y


def _rev_bias(rel_bias_cols, t):
    n = jnp.arange(t, dtype=jnp.int32)
    nf = jnp.maximum(n, 1).astype(F32)
    large = MAX_EXACT + (jnp.log(nf / MAX_EXACT) / math.log(MAX_DISTANCE / MAX_EXACT)
                         * (N_BUCKETS - MAX_EXACT)).astype(jnp.int32)
    bucket = jnp.where(n < MAX_EXACT, n, jnp.minimum(large, N_BUCKETS - 1))
    bv = rel_bias_cols[bucket].T.astype(F32)
    return jnp.pad(bv[:, ::-1], ((0, 0), (0, RB_LEN - t)))


def _cumsum_kernel(x_ref, o_ref, *, t, tc):
    row = lax.broadcasted_iota(jnp.int32, (tc, tc), 0)
    col = lax.broadcasted_iota(jnp.int32, (tc, tc), 1)
    tri = jnp.where(col <= row, 1.0, 0.0).astype(F32)
    carry = jnp.zeros((1, LANES), F32)
    for c in range(t // tc):
        pc = jnp.dot(tri, x_ref[c * tc:(c + 1) * tc, :], preferred_element_type=F32,
                     precision=lax.Precision.HIGHEST) + carry
        o_ref[c * tc:(c + 1) * tc, :] = pc
        carry = pc[tc - 1:tc, :]


def cumsum_time(x3):
    b, t, w = x3.shape
    return pl.pallas_call(
        functools.partial(_cumsum_kernel, t=t, tc=256),
        grid=(b,),
        in_specs=[pl.BlockSpec((None, t, w), lambda i: (i, 0, 0))],
        out_specs=pl.BlockSpec((None, t, w), lambda i: (i, 0, 0)),
        out_shape=jax.ShapeDtypeStruct((b, t, w), F32),
        compiler_params=_cparams(("parallel",)),
        name="logf_cumsum",
    )(x3)


def _fox_kernel(q_ref, k_ref, v_ref, f_ref, o_ref, m_ref, l_ref, acc_ref, *, tq):
    qi = pl.program_id(2)
    q = (q_ref[...] * SCALE).astype(BF16)
    _init_state(m_ref, l_ref, acc_ref)

    def chunk(kj, masked):
        ks = pl.multiple_of(kj * tq, tq)
        k = k_ref[pl.ds(ks, tq), :].astype(BF16)
        v = v_ref[pl.ds(ks, tq), :].astype(BF16)
        s = _dot_nt(q, k) - f_ref[:, pl.ds(ks, tq)]
        if masked:
            s = jnp.where(_causal_tile(tq), s, NEG_INF)
        _online_update(s, v, m_ref, l_ref, acc_ref)

    def body(kj, c):
        chunk(kj, False)
        return c

    lax.fori_loop(0, qi, body, 0)
    chunk(qi, True)
    o_ref[...] = (acc_ref[...] / l_ref[...]).astype(o_ref.dtype)


def fox_prompt(a3, f_rows):
    b, t, _ = a3.shape
    tq = FOX_TQ
    return pl.pallas_call(
        functools.partial(_fox_kernel, tq=tq),
        grid=(b, FOX_HEADS, t // tq),
        in_specs=[pl.BlockSpec((None, tq, HEAD_DIM), lambda bi, h, qi: (bi, qi, COL_FQ + h)),
                  pl.BlockSpec((None, t, HEAD_DIM), lambda bi, h, qi: (bi, 0, COL_FK + h)),
                  pl.BlockSpec((None, t, HEAD_DIM), lambda bi, h, qi: (bi, 0, COL_FV + h)),
                  pl.BlockSpec((None, None, 1, t), lambda bi, h, qi: (bi, h, 0, 0))],
        out_specs=pl.BlockSpec((None, tq, HEAD_DIM), lambda bi, h, qi: (bi, qi, h)),
        out_shape=jax.ShapeDtypeStruct((b, t, FW), BF16),
        scratch_shapes=[pltpu.VMEM((tq, 1), F32), pltpu.VMEM((tq, 1), F32),
                        pltpu.VMEM((tq, HEAD_DIM), F32)],
        compiler_params=_cparams(("parallel", "parallel", "arbitrary")),
        name="fox_prompt",
    )(a3, a3, a3, f_rows)


def _diff_lambda(lam_ref, lam_init):
    lp = lam_ref[...]
    a = jnp.sum(lp[0:1, :] * lp[1:2, :], axis=-1, keepdims=True)
    b = jnp.sum(lp[2:3, :] * lp[3:4, :], axis=-1, keepdims=True)
    return jnp.exp(a) - jnp.exp(b) + lam_init


def _diff_kernel(q_ref, k_ref, v_ref, rb_ref, lam_ref, g_ref, o_ref,
                 tiles_ref, m1, l1, a1, m2, l2, a2, *, tq, t, n_tiles, lam_init):
    bi = pl.program_id(1)
    qi = pl.program_id(2)

    @pl.when((bi == 0) & (qi == 0))
    def _():
        _build_bias_tiles(lambda lo, hi: rb_ref[:, lo:hi], tiles_ref, n_tiles, tq, t)

    q1 = (q_ref[:, :HEAD_DIM] * SCALE).astype(BF16)
    q2 = (q_ref[:, HEAD_DIM:] * SCALE).astype(BF16)
    _init_state(m1, l1, a1)
    _init_state(m2, l2, a2)
    far_bias = rb_ref[:, t - 1 - MAX_DISTANCE:t - MAX_DISTANCE]

    def chunk(kj, bias):
        ks = pl.multiple_of(kj * tq, tq)
        k = k_ref[pl.ds(ks, tq), :].astype(BF16)
        v = v_ref[pl.ds(ks, tq), :].astype(BF16)
        _online_update(_dot_nt(q1, k[:, :HEAD_DIM]) + bias, v, m1, l1, a1)
        _online_update(_dot_nt(q2, k[:, HEAD_DIM:]) + bias, v, m2, l2, a2)

    lo = jnp.maximum(qi - (n_tiles - 1), 0)

    def far_body(kj, c):
        chunk(kj, far_bias)
        return c

    def near_body(kj, c):
        chunk(kj, tiles_ref[qi - kj])
        return c

    lax.fori_loop(0, lo, far_body, 0)
    lax.fori_loop(lo, qi + 1, near_body, 0)
    lam = _diff_lambda(lam_ref, lam_init)
    o = a1[...] / l1[...] - lam * (a2[...] / l2[...])
    o_ref[...] = (_rms(o, g_ref[...]) * (1.0 - lam_init)).astype(o_ref.dtype)


def diff_prompt(a3, rb_diff, lam_p, subln_g, lam_init):
    b, t, _ = a3.shape
    tq = DIFF_TQ
    n_tiles = _n_bias_tiles(t, tq)
    w = DIFF_VDIM
    st = lambda d: pltpu.VMEM((tq, d), F32)
    return pl.pallas_call(
        functools.partial(_diff_kernel, tq=tq, t=t, n_tiles=n_tiles, lam_init=lam_init),
        grid=(DIFF_HEADS, b, t // tq),
        in_specs=[pl.BlockSpec((None, tq, w), lambda h, bi, qi: (bi, qi, COL_DQ // 2 + h)),
                  pl.BlockSpec((None, t, w), lambda h, bi, qi: (bi, 0, COL_DK // 2 + h)),
                  pl.BlockSpec((None, t, w), lambda h, bi, qi: (bi, 0, COL_DV // 2 + h)),
                  pl.BlockSpec((None, 1, RB_LEN), lambda h, bi, qi: (h, 0, 0)),
                  pl.BlockSpec((4, HEAD_DIM), lambda h, bi, qi: (0, 0)),
                  pl.BlockSpec((1, w), lambda h, bi, qi: (0, 0))],
        out_specs=pl.BlockSpec((None, tq, w), lambda h, bi, qi: (bi, qi, h)),
        out_shape=jax.ShapeDtypeStruct((b, t, DW), BF16),
        scratch_shapes=[pltpu.VMEM((n_tiles, tq, tq), F32),
                        st(1), st(1), st(w), st(1), st(1), st(w)],
        compiler_params=_cparams(("arbitrary", "arbitrary", "arbitrary")),
        name="diff_prompt",
    )(a3, a3, a3, rb_diff.reshape(DIFF_HEADS, 1, RB_LEN), lam_p, subln_g.reshape(1, w))


def _gelu_tanh(x):
    return 0.5 * x * (1.0 + jnp.tanh(math.sqrt(2.0 / math.pi) * (x + 0.044715 * (x * x * x))))


def _compress_kernel(x_ref, w1_ref, w2_ref, pos_ref, o_ref, *, nb):
    acc = jnp.zeros((nb, HEAD_DIM), F32)
    for l in range(CMP_BLOCK):
        xl = x_ref[pl.ds(l, nb, stride=CMP_BLOCK), :] + pos_ref[l:l + 1, :]
        acc = acc + jnp.dot(xl.astype(BF16), w1_ref[l], preferred_element_type=F32)
    h = _gelu_tanh(acc)
    o_ref[...] = jnp.dot(h.astype(BF16), w2_ref[...], preferred_element_type=F32)


def compress_prompt(a3, w1, w2, pos):
    b, t, _ = a3.shape
    nb = t // CMP_BLOCK
    return pl.pallas_call(
        functools.partial(_compress_kernel, nb=nb),
        grid=(b, 2, NSA_KV_HEADS),
        in_specs=[pl.BlockSpec((None, t, HEAD_DIM), lambda bi, s, k: (bi, 0, COL_NKV + s * NSA_KV_HEADS + k)),
                  pl.BlockSpec((None, CMP_BLOCK, HEAD_DIM, HEAD_DIM), lambda bi, s, k: (s, 0, 0, 0)),
                  pl.BlockSpec((None, HEAD_DIM, HEAD_DIM), lambda bi, s, k: (s, 0, 0)),
                  pl.BlockSpec((None, CMP_BLOCK, HEAD_DIM), lambda bi, s, k: (s, 0, 0))],
        out_specs=pl.BlockSpec((None, None, None, nb, HEAD_DIM), lambda bi, s, k: (bi, s, k, 0, 0)),
        out_shape=jax.ShapeDtypeStruct((b, 2, NSA_KV_HEADS, nb, HEAD_DIM), F32),
        compiler_params=_cparams(("parallel", "parallel", "parallel")),
        name="nsa_compress",
    )(a3, w1, w2, pos)


def _masked_softmax_rows(lc, valid):
    lcm = jnp.where(valid, lc, NEG_INF)
    e = jnp.where(valid, jnp.exp(lcm - jnp.max(lcm, axis=-1, keepdims=True)), 0.0)
    den = jnp.sum(e, axis=-1, keepdims=True)
    return e / jnp.where(den > 0.0, den, 1.0)


def _select_blocks(score, blk, t_pos, n_blk, n_sel):
    cur = t_pos // CMP_BLOCK
    forced = (blk == 0) | (blk == cur) | (blk == cur - 1)
    score = jnp.where(forced, FORCE_SCORE, score)
    score = jnp.where(blk * CMP_BLOCK <= t_pos, score, -1.0)
    rank = jnp.zeros(score.shape, jnp.int32)
    for j in range(n_blk):
        cj = score[:, j:j + 1]
        ahead = (cj > score) | ((cj == score) & (blk > j))
        rank = rank + jnp.where(ahead, 1, 0)
    return rank < n_sel


def _cmp_select_kernel(q_ref, kc_ref, vc_ref, o_ref, sel_ref, *, tq, nb):
    qi = pl.program_id(2)
    blk = lax.broadcasted_iota(jnp.int32, (tq, nb), 1)
    t_pos = qi * tq + lax.broadcasted_iota(jnp.int32, (tq, nb), 0)
    valid = blk * CMP_BLOCK + (CMP_BLOCK - 1) <= t_pos
    kc = kc_ref[...].astype(BF16)
    vc = vc_ref[...].astype(BF16)
    score = jnp.zeros((tq, nb), F32)
    for g in range(NSA_GROUP):
        q = (q_ref[:, g * HEAD_DIM:(g + 1) * HEAD_DIM] * SCALE).astype(BF16)
        pc = _masked_softmax_rows(_dot_nt(q, kc), valid)
        score = score + pc
        o_ref[:, g * HEAD_DIM:(g + 1) * HEAD_DIM] = jnp.dot(pc.astype(BF16), vc, preferred_element_type=F32)
    sel = _select_blocks(score, blk, t_pos, nb, min(N_SELECT, nb))
    sel_ref[...] = jnp.where(sel, 1.0, 0.0)


def cmp_select_prompt(a3, cmp_kv):
    b, t, _ = a3.shape
    tq = NSA_TQ
    nb = t // CMP_BLOCK
    gw = NSA_GROUP * HEAD_DIM
    return pl.pallas_call(
        functools.partial(_cmp_select_kernel, tq=tq, nb=nb),
        grid=(b, NSA_KV_HEADS, t // tq),
        in_specs=[pl.BlockSpec((None, tq, gw), lambda bi, k, qi: (bi, qi, COL_NQ // NSA_GROUP + k)),
                  pl.BlockSpec((None, None, None, nb, HEAD_DIM), lambda bi, k, qi: (bi, 0, k, 0, 0)),
                  pl.BlockSpec((None, None, None, nb, HEAD_DIM), lambda bi, k, qi: (bi, 1, k, 0, 0))],
        out_specs=[pl.BlockSpec((None, tq, gw), lambda bi, k, qi: (bi, qi, k)),
                   pl.BlockSpec((None, None, tq, nb), lambda bi, k, qi: (bi, k, qi, 0))],
        out_shape=[jax.ShapeDtypeStruct((b, t, NSA_HEADS * HEAD_DIM), F32),
                   jax.ShapeDtypeStruct((b, NSA_KV_HEADS, t, nb), F32)],
        compiler_params=_cparams(("parallel", "parallel", "parallel")),
        name="nsa_cmp_select",
    )(a3, cmp_kv, cmp_kv)


def _slc_win_kernel(q_ref, ks_ref, vs_ref, kw_ref, vw_ref, sel_ref, rb_ref, ocmp_ref, gate_ref, o_ref,
                    tiles_ref, ms, ls, accs, mw, lw, accw, *, tq, t, nb, n_tiles):
    k_idx = pl.program_id(0)
    bi = pl.program_id(1)
    qi = pl.program_id(2)
    n_win_tiles = WINDOW // tq + 1

    @pl.when((bi == 0) & (qi == 0))
    def _():
        for g in range(NSA_GROUP):
            _build_bias_tiles(lambda lo, hi, g=g: rb_ref[g:g + 1, lo:hi], tiles_ref, n_tiles, tq, t, lead=(g,))

    qs = [(q_ref[:, g * HEAD_DIM:(g + 1) * HEAD_DIM] * SCALE).astype(BF16) for g in range(NSA_GROUP)]
    for g in range(NSA_GROUP):
        _init_state(ms.at[g], ls.at[g], accs.at[g])
        _init_state(mw.at[g], lw.at[g], accw.at[g])
    selb = sel_ref[...].astype(BF16)
    blk_row = lax.broadcasted_iota(jnp.int32, (nb, tq), 0)
    key_col = lax.broadcasted_iota(jnp.int32, (nb, tq), 1)

    def slc_chunk(kj, bias_of):
        ks0 = pl.multiple_of(kj * tq, tq)
        expand = jnp.where((ks0 + key_col) // CMP_BLOCK == blk_row, 1.0, 0.0).astype(BF16)
        chosen = jnp.dot(selb, expand, preferred_element_type=F32) > 0.5
        k = ks_ref[pl.ds(ks0, tq), :].astype(BF16)
        v = vs_ref[pl.ds(ks0, tq), :].astype(BF16)
        for g in range(NSA_GROUP):
            s = jnp.where(chosen, _dot_nt(qs[g], k) + bias_of(g, kj), NEG_INF)
            _online_update(s, v, ms.at[g], ls.at[g], accs.at[g])

    lo = jnp.maximum(qi - (n_tiles - 1), 0)

    def far_body(kj, c):
        slc_chunk(kj, lambda g, kj: rb_ref[g:g + 1, t - 1 - MAX_DISTANCE:t - MAX_DISTANCE])
        return c

    def near_body(kj, c):
        slc_chunk(kj, lambda g, kj: tiles_ref[g, qi - kj])
        return c

    lax.fori_loop(0, lo, far_body, 0)
    lax.fori_loop(lo, qi + 1, near_body, 0)

    row = lax.broadcasted_iota(jnp.int32, (tq, tq), 0)
    col = lax.broadcasted_iota(jnp.int32, (tq, tq), 1)

    def win_chunk(d):
        ks0 = pl.multiple_of((qi - d) * tq, tq)
        k = kw_ref[pl.ds(ks0, tq), :].astype(BF16)
        v = vw_ref[pl.ds(ks0, tq), :].astype(BF16)
        for g in range(NSA_GROUP):
            s = _dot_nt(qs[g], k) + tiles_ref[g, d]
            if d == n_win_tiles - 1:
                s = jnp.where(d * tq + row - col < WINDOW, s, NEG_INF)
            _online_update(s, v, mw.at[g], lw.at[g], accw.at[g])

    win_chunk(0)
    for d in range(1, n_win_tiles):
        pl.when(qi >= d)(functools.partial(win_chunk, d))

    gates = gate_ref[...]
    lane = lax.broadcasted_iota(jnp.int32, gates.shape, 1)
    for g in range(NSA_GROUP):
        base = N_FF + (k_idx * NSA_GROUP + g) * 3
        gate = lambda j: jnp.sum(jnp.where(lane == base + j, gates, 0.0), axis=-1, keepdims=True)
        sl = slice(g * HEAD_DIM, (g + 1) * HEAD_DIM)
        o = (gate(0) * ocmp_ref[:, sl] + gate(1) * (accs[g] / ls[g]) + gate(2) * (accw[g] / lw[g]))
        o_ref[:, sl] = o.astype(o_ref.dtype)


def slc_win_prompt(a3, sel, rb_nsa, o_cmp, s_act):
    b, t, _ = a3.shape
    tq = NSA_TQ
    nb = t // CMP_BLOCK
    n_tiles = _n_bias_tiles(t, tq)
    assert WINDOW % tq == 0 and WINDOW // tq + 1 <= n_tiles
    gw = NSA_GROUP * HEAD_DIM
    kv = lambda stream: pl.BlockSpec(
        (None, t, HEAD_DIM), lambda k, bi, qi: (bi, 0, COL_NKV + stream * NSA_KV_HEADS + k))
    st = lambda d: pltpu.VMEM((NSA_GROUP, tq, d), F32)
    return pl.pallas_call(
        functools.partial(_slc_win_kernel, tq=tq, t=t, nb=nb, n_tiles=n_tiles),
        grid=(NSA_KV_HEADS, b, t // tq),
        in_specs=[pl.BlockSpec((None, tq, gw), lambda k, bi, qi: (bi, qi, COL_NQ // NSA_GROUP + k)),
                  kv(2), kv(3), kv(4), kv(5),
                  pl.BlockSpec((None, None, tq, nb), lambda k, bi, qi: (bi, k, qi, 0)),
                  pl.BlockSpec((None, NSA_GROUP, RB_LEN), lambda k, bi, qi: (k, 0, 0)),
                  pl.BlockSpec((None, tq, gw), lambda k, bi, qi: (bi, qi, k)),
                  pl.BlockSpec((None, tq, LANES), lambda k, bi, qi: (bi, qi, 0))],
        out_specs=pl.BlockSpec((None, tq, gw), lambda k, bi, qi: (bi, qi, k)),
        out_shape=jax.ShapeDtypeStruct((b, t, NSA_HEADS * HEAD_DIM), BF16),
        scratch_shapes=[pltpu.VMEM((NSA_GROUP, n_tiles, tq, tq), F32),
                        st(1), st(1), st(HEAD_DIM), st(1), st(1), st(HEAD_DIM)],
        compiler_params=_cparams(("arbitrary", "arbitrary", "arbitrary")),
        name="nsa_slc_win",
    )(a3, a3, a3, a3, a3, sel, rb_nsa.reshape(NSA_KV_HEADS, NSA_GROUP, RB_LEN), o_cmp, s_act)


DEC_PAGES_PER_STEP = 8
CMP_PAGES_PER_STEP = 16
NEW_ROWS = 16
QROWS = 8
LF_ROWS = 16
HI = lax.Precision.HIGHEST


def _page_specs(pp, block, layer, tail):
    def spec(jj):
        return pl.BlockSpec(block, lambda b, i, pt: (pt[b, i * pp + jj], layer) + tail)
    return [spec(jj) for jj in range(pp)]


def _new_row_live():
    return lax.broadcasted_iota(jnp.int32, (QROWS, NEW_ROWS), 1) == 0


def _fox_dec_kernel(pt_ref, *refs, pp, n_steps):
    kv_refs, lf_refs = refs[:pp], refs[pp:2 * pp]
    q_ref, kn_ref, vn_ref, lfn_ref, o_ref, m_ref, l_ref, acc_ref, carry_ref = refs[2 * pp:]
    i = pl.program_id(1)

    @pl.when(i == 0)
    def _():
        for h in range(FOX_HEADS):
            _init_state(m_ref.at[h], l_ref.at[h], acc_ref.at[h])
        carry_ref[...] = jnp.zeros(carry_ref.shape, F32)

    row = lax.broadcasted_iota(jnp.int32, (PAGE_SIZE, PAGE_SIZE), 0)
    col = lax.broadcasted_iota(jnp.int32, (PAGE_SIZE, PAGE_SIZE), 1)
    tri_u = jnp.where(row <= col, 1.0, 0.0).astype(F32)
    qs = [(q_ref[h] * SCALE).astype(BF16) for h in range(FOX_HEADS)]
    carry = carry_ref[...]
    fts = []
    for jj in range(pp):
        ft = jnp.dot(lf_refs[jj][...], tri_u, preferred_element_type=F32, precision=HI) + carry
        carry = jnp.broadcast_to(ft[:, PAGE_SIZE - 1:PAGE_SIZE], carry_ref.shape)
        fts.append(ft)
    carry_ref[...] = carry
    f_all = jnp.concatenate(fts, axis=1)
    for h in range(FOX_HEADS):
        rows = lambda jj, kv: kv_refs[jj][pl.ds(2 * h + kv, PAGE_SIZE, stride=2 * FOX_HEADS), :].astype(BF16)
        k = jnp.concatenate([rows(jj, 0) for jj in range(pp)], axis=0)
        v = jnp.concatenate([rows(jj, 1) for jj in range(pp)], axis=0)
        _online_update(_dot_nt(qs[h], k) - f_all[h:h + 1, :], v, m_ref.at[h], l_ref.at[h], acc_ref.at[h])

    @pl.when(i == n_steps - 1)
    def _():
        f_new = carry_ref[...] + lfn_ref[...]
        live = _new_row_live()
        for h in range(FOX_HEADS):
            s = _dot_nt(qs[h], kn_ref[h].astype(BF16)) - f_new[h:h + 1, :NEW_ROWS]
            _online_update(jnp.where(live, s, NEG_INF), vn_ref[h].astype(BF16),
                           m_ref.at[h], l_ref.at[h], acc_ref.at[h])
            o_ref[h] = acc_ref[h] / l_ref[h]


def fox_decode(page_table, cache_kv, lf_t, layer, q, k_new, v_new, lf_new):
    bsz, n_pages = page_table.shape
    pp = DEC_PAGES_PER_STEP
    n_steps = n_pages // pp
    hq = pl.BlockSpec((None, FOX_HEADS, QROWS, HEAD_DIM), lambda b, i, pt: (b, 0, 0, 0))
    hn = pl.BlockSpec((None, FOX_HEADS, NEW_ROWS, HEAD_DIM), lambda b, i, pt: (b, 0, 0, 0))
    st = lambda d: pltpu.VMEM((FOX_HEADS, QROWS, d), F32)
    grid_spec = pltpu.PrefetchScalarGridSpec(
        num_scalar_prefetch=1, grid=(bsz, n_steps),
        in_specs=(_page_specs(pp, (None, None, PAGE_SIZE * FOX_HEADS * 2, HEAD_DIM), layer, (0, 0))
                  + _page_specs(pp, (None, None, LF_ROWS, PAGE_SIZE), layer, (0, 0))
                  + [hq, hn, hn, pl.BlockSpec((None, LF_ROWS, LANES), lambda b, i, pt: (b, 0, 0))]),
        out_specs=hq,
        scratch_shapes=[st(1), st(1), st(HEAD_DIM), pltpu.VMEM((LF_ROWS, LANES), F32)])
    return pl.pallas_call(
        functools.partial(_fox_dec_kernel, pp=pp, n_steps=n_steps),
        grid_spec=grid_spec,
        out_shape=jax.ShapeDtypeStruct((bsz, FOX_HEADS, QROWS, HEAD_DIM), F32),
        compiler_params=_cparams(("parallel", "arbitrary")),
        name="fox_decode",
    )(page_table, *([cache_kv] * pp), *([lf_t] * pp), q, k_new, v_new, lf_new)


def _diff_dec_kernel(pt_ref, *refs, pp, n_steps, lam_init):
    kv_refs, b_refs = refs[:pp], refs[pp:2 * pp]
    q_ref, kn_ref, vn_ref, bn_ref, lam_ref, g_ref, o_ref, m_ref, l_ref, acc_ref = refs[2 * pp:]
    i = pl.program_id(1)

    @pl.when(i == 0)
    def _():
        for h in range(DIFF_HEADS):
            for c in range(2):
                _init_state(m_ref.at[h, c], l_ref.at[h, c], acc_ref.at[h, c])

    qs = [[(q_ref[h, c] * SCALE).astype(BF16) for c in range(2)] for h in range(DIFF_HEADS)]
    bias = jnp.concatenate([b_refs[jj][...] for jj in range(pp)], axis=1)
    for h in range(DIFF_HEADS):
        rows = lambda jj, slot: kv_refs[jj][pl.ds(slot, PAGE_SIZE, stride=4 * DIFF_HEADS), :].astype(BF16)
        v = jnp.concatenate([jnp.concatenate([rows(jj, (2 + c) * DIFF_HEADS + h) for c in range(2)], axis=1)
                             for jj in range(pp)], axis=0)
        for c in range(2):
            k = jnp.concatenate([rows(jj, c * DIFF_HEADS + h) for jj in range(pp)], axis=0)
            _online_update(_dot_nt(qs[h][c], k) + bias[h:h + 1, :], v,
                           m_ref.at[h, c], l_ref.at[h, c], acc_ref.at[h, c])

    @pl.when(i == n_steps - 1)
    def _():
        live = _new_row_live()
        lam = _diff_lambda(lam_ref, lam_init)
        for h in range(DIFF_HEADS):
            v = vn_ref[h].astype(BF16)
            for c in range(2):
                s = _dot_nt(qs[h][c], kn_ref[h, c].astype(BF16)) + bn_ref[h:h + 1, :NEW_ROWS]
                _online_update(jnp.where(live, s, NEG_INF), v, m_ref.at[h, c], l_ref.at[h, c], acc_ref.at[h, c])
            o = acc_ref[h, 0] / l_ref[h, 0] - lam * (acc_ref[h, 1] / l_ref[h, 1])
            o_ref[h] = _rms(o, g_ref[...]) * (1.0 - lam_init)


def diff_decode(page_table, cache_kv, bias_t, layer, q, k_new, v_new, bias_new, lam_p, subln_g, lam_init):
    bsz, n_pages = page_table.shape
    pp = DEC_PAGES_PER_STEP
    n_steps = n_pages // pp
    w = DIFF_VDIM

    def bias_spec(jj):
        return pl.BlockSpec((None, QROWS, PAGE_SIZE), lambda b, i, pt: (i * pp + jj, 0, 0))

    const = lambda shape: pl.BlockSpec(shape, lambda b, i, pt: (0,) * len(shape))
    st = lambda d: pltpu.VMEM((DIFF_HEADS, 2, QROWS, d), F32)
    out_spec = pl.BlockSpec((None, DIFF_HEADS, QROWS, w), lambda b, i, pt: (b, 0, 0, 0))
    grid_spec = pltpu.PrefetchScalarGridSpec(
        num_scalar_prefetch=1, grid=(bsz, n_steps),
        in_specs=(_page_specs(pp, (None, None, PAGE_SIZE * 4 * DIFF_HEADS, HEAD_DIM), layer, (0, 0))
                  + [bias_spec(jj) for jj in range(pp)]
                  + [pl.BlockSpec((None, DIFF_HEADS, 2, QROWS, HEAD_DIM), lambda b, i, pt: (b, 0, 0, 0, 0)),
                     pl.BlockSpec((None, DIFF_HEADS, 2, NEW_ROWS, HEAD_DIM), lambda b, i, pt: (b, 0, 0, 0, 0)),
                     pl.BlockSpec((None, DIFF_HEADS, NEW_ROWS, w), lambda b, i, pt: (b, 0, 0, 0)),
                     const((QROWS, LANES)), const((4, HEAD_DIM)), const((1, w))]),
        out_specs=out_spec,
        scratch_shapes=[st(1), st(1), st(w)])
    return pl.pallas_call(
        functools.partial(_diff_dec_kernel, pp=pp, n_steps=n_steps, lam_init=lam_init),
        grid_spec=grid_spec,
        out_shape=jax.ShapeDtypeStruct((bsz, DIFF_HEADS, QROWS, w), F32),
        compiler_params=_cparams(("parallel", "arbitrary")),
        name="diff_decode",
    )(page_table, *([cache_kv] * pp), *([bias_t] * pp), q, k_new, v_new, bias_new, lam_p, subln_g.reshape(1, w))


def _cmp_dec_kernel(pt_ref, *refs, pp):
    kv_refs = refs[:pp]
    pos_ref, w1_ref, w2_ref, o_ref, g_ref = refs[pp:]
    rows = pp * PAGE_SIZE
    for s in range(2):
        for k in range(NSA_KV_HEADS):
            for jj in range(pp):
                g_ref[s, pl.ds(k * rows + jj * PAGE_SIZE, PAGE_SIZE), :] = (
                    kv_refs[jj][pl.ds(k * 4 + s, PAGE_SIZE, stride=4 * NSA_KV_HEADS), :] + pos_ref[s])
    per_head = 2 * pp
    nrow = NSA_KV_HEADS * per_head
    for s in range(2):
        acc = jnp.zeros((nrow, HEAD_DIM), F32)
        for l in range(CMP_BLOCK):
            xl = g_ref.at[s][pl.ds(l, nrow, stride=CMP_BLOCK), :]
            acc = acc + jnp.dot(xl.astype(BF16), w1_ref[s, l], preferred_element_type=F32)
        out = jnp.dot(_gelu_tanh(acc).astype(BF16), w2_ref[s], preferred_element_type=F32)
        for k in range(NSA_KV_HEADS):
            o_ref[s, k] = out[k * per_head:(k + 1) * per_head, :]


def compress_decode(page_table, cache_kv, layer, pos2, w1, w2):
    bsz, n_pages = page_table.shape
    pp = min(CMP_PAGES_PER_STEP, n_pages)
    per_page = PAGE_SIZE // CMP_BLOCK
    const = lambda shape: pl.BlockSpec(shape, lambda b, i, pt: (0,) * len(shape))
    grid_spec = pltpu.PrefetchScalarGridSpec(
        num_scalar_prefetch=1, grid=(bsz, n_pages // pp),
        in_specs=(_page_specs(pp, (None, None, PAGE_SIZE * NSA_KV_HEADS * 4, HEAD_DIM), layer, (0, 0))
                  + [const((2, PAGE_SIZE, HEAD_DIM)), const((2, CMP_BLOCK, HEAD_DIM, HEAD_DIM)),
                     const((2, HEAD_DIM, HEAD_DIM))]),
        out_specs=pl.BlockSpec((None, 2, NSA_KV_HEADS, per_page * pp, HEAD_DIM), lambda b, i, pt: (b, 0, 0, i, 0)),
        scratch_shapes=[pltpu.VMEM((2, NSA_KV_HEADS * pp * PAGE_SIZE, HEAD_DIM), F32)])
    return pl.pallas_call(
        functools.partial(_cmp_dec_kernel, pp=pp),
        grid_spec=grid_spec,
        out_shape=jax.ShapeDtypeStruct((bsz, 2, NSA_KV_HEADS, per_page * n_pages, HEAD_DIM), F32),
        compiler_params=_cparams(("parallel", "arbitrary")),
        name="nsa_compress_decode",
    )(page_table, *([cache_kv] * pp), pos2, w1, w2)


def _cmp_sel_dec_kernel(q_ref, kc_ref, vc_ref, o_ref, idx_ref, *, n_blk, n_pick):
    q = (q_ref[...] * SCALE).astype(BF16)
    lc = _dot_nt(q, kc_ref[...].astype(BF16))
    e = jnp.exp(lc - jnp.max(lc, axis=-1, keepdims=True))
    pc = e / jnp.sum(e, axis=-1, keepdims=True)
    o_ref[...] = jnp.dot(pc.astype(BF16), vc_ref[...].astype(BF16), preferred_element_type=F32)
    rowi = lax.broadcasted_iota(jnp.int32, (QROWS, n_blk), 0)
    blk = lax.broadcasted_iota(jnp.int32, (QROWS, n_blk), 1)
    score = jnp.broadcast_to(jnp.sum(jnp.where(rowi < NSA_GROUP, pc, 0.0), axis=0, keepdims=True), (QROWS, n_blk))
    score = jnp.where((blk == 0) | (blk == n_blk - 1), FORCE_SCORE, score)
    rank = jnp.zeros((QROWS, n_blk), jnp.int32)
    for j in range(n_blk):
        cj = score[:, j:j + 1]
        rank = rank + jnp.where((cj > score) | ((cj == score) & (blk > j)), 1, 0)
    lane = lax.broadcasted_iota(jnp.int32, (QROWS, LANES), 1)
    blk_f = blk.astype(F32)
    picked = jnp.zeros((QROWS, LANES), F32)
    for r in range(n_pick):
        v = jnp.sum(jnp.where(rank == r, blk_f, 0.0), axis=-1, keepdims=True)
        picked = jnp.where(lane == r, v, picked)
    idx_ref[...] = picked.astype(jnp.int32)


def cmp_select_decode(q8, cmp_kv):
    bsz = q8.shape[0]
    n_blk = cmp_kv.shape[3]
    n_pick = N_SELECT - 1
    assert n_blk >= n_pick
    hspec = pl.BlockSpec((None, None, QROWS, HEAD_DIM), lambda b, k: (b, k, 0, 0))
    return pl.pallas_call(
        functools.partial(_cmp_sel_dec_kernel, n_blk=n_blk, n_pick=n_pick),
        grid=(bsz, NSA_KV_HEADS),
        in_specs=[hspec,
                  pl.BlockSpec((None, None, None, n_blk, HEAD_DIM), lambda b, k: (b, 0, k, 0, 0)),
                  pl.BlockSpec((None, None, None, n_blk, HEAD_DIM), lambda b, k: (b, 1, k, 0, 0))],
        out_specs=[hspec, pl.BlockSpec((None, None, QROWS, LANES), lambda b, k: (b, k, 0, 0))],
        out_shape=[jax.ShapeDtypeStruct((bsz, NSA_KV_HEADS, QROWS, HEAD_DIM), F32),
                   jax.ShapeDtypeStruct((bsz, NSA_KV_HEADS, QROWS, LANES), jnp.int32)],
        compiler_params=_cparams(("parallel", "parallel")),
        name="nsa_cmp_select_decode",
    )(q8, cmp_kv, cmp_kv)


def _slc_dec_kernel(pt_ref, sel_ref, *refs, n_pick):
    kv_refs, b_refs = refs[:NSA_KV_HEADS], refs[NSA_KV_HEADS:2 * NSA_KV_HEADS]
    q_ref, kn_ref, vn_ref, bn_ref, o_ref, m_ref, l_ref, acc_ref = refs[2 * NSA_KV_HEADS:]
    i = pl.program_id(1)

    @pl.when(i == 0)
    def _():
        for k in range(NSA_KV_HEADS):
            _init_state(m_ref.at[k], l_ref.at[k], acc_ref.at[k])

    qs = [(q_ref[k] * SCALE).astype(BF16) for k in range(NSA_KV_HEADS)]
    for k in range(NSA_KV_HEADS):
        rows = lambda s: kv_refs[k][pl.ds(k * 4 + s, CMP_BLOCK, stride=4 * NSA_KV_HEADS), :].astype(BF16)
        ks, vs = rows(2), rows(3)
        _online_update(_dot_nt(qs[k], ks) + b_refs[k][...], vs, m_ref.at[k], l_ref.at[k], acc_ref.at[k])

    @pl.when(i == n_pick - 1)
    def _():
        live = _new_row_live()
        for k in range(NSA_KV_HEADS):
            s = _dot_nt(qs[k], kn_ref[k].astype(BF16)) + bn_ref[k][:, :NEW_ROWS]
            _online_update(jnp.where(live, s, NEG_INF), vn_ref[k].astype(BF16), m_ref.at[k], l_ref.at[k], acc_ref.at[k])
            o_ref[k] = acc_ref[k] / l_ref[k]


def slc_decode(page_table, sel, cache_kv, layer, q8, k_new, v_new, bias_blk, bias_new):
    bsz = q8.shape[0]
    n_pick = N_SELECT - 1
    per_page = PAGE_SIZE // CMP_BLOCK

    def kv_spec(k):
        def idx(b, i, pt, sel):
            blk = sel[b, k * N_SELECT + i]
            return (pt[b, blk // per_page], layer, blk % per_page, 0)
        return pl.BlockSpec((None, None, CMP_BLOCK * NSA_KV_HEADS * 4, HEAD_DIM), idx)

    def bias_spec(k):
        return pl.BlockSpec((None, None, QROWS, CMP_BLOCK), lambda b, i, pt, sel: (k, sel[b, k * N_SELECT + i], 0, 0))

    hq = pl.BlockSpec((None, NSA_KV_HEADS, QROWS, HEAD_DIM), lambda b, i, pt, sel: (b, 0, 0, 0))
    hn = pl.BlockSpec((None, NSA_KV_HEADS, NEW_ROWS, HEAD_DIM), lambda b, i, pt, sel: (b, 0, 0, 0))
    st = lambda d: pltpu.VMEM((NSA_KV_HEADS, QROWS, d), F32)
    grid_spec = pltpu.PrefetchScalarGridSpec(
        num_scalar_prefetch=2, grid=(bsz, n_pick),
        in_specs=([kv_spec(k) for k in range(NSA_KV_HEADS)] + [bias_spec(k) for k in range(NSA_KV_HEADS)]
                  + [hq, hn, hn, pl.BlockSpec((NSA_KV_HEADS, QROWS, LANES), lambda b, i, pt, sel: (0, 0, 0))]),
        out_specs=hq,
        scratch_shapes=[st(1), st(1), st(HEAD_DIM)])
    return pl.pallas_call(
        functools.partial(_slc_dec_kernel, n_pick=n_pick),
        grid_spec=grid_spec,
        out_shape=jax.ShapeDtypeStruct((bsz, NSA_KV_HEADS, QROWS, HEAD_DIM), F32),
        compiler_params=_cparams(("parallel", "arbitrary")),
        name="nsa_slc_decode",
    )(page_table, sel, *([cache_kv] * NSA_KV_HEADS), *([bias_blk] * NSA_KV_HEADS), q8, k_new, v_new, bias_new)


def _win_dec_kernel(q_ref, st_ref, kn_ref, vn_ref, wb_ref, bn_ref, oc_ref, os_ref, g_ref, o_ref):
    live = _new_row_live()
    for k in range(NSA_KV_HEADS):
        q = (q_ref[k] * SCALE).astype(BF16)
        s = _dot_nt(q, st_ref[:, 0, k, :].astype(BF16)) + wb_ref[k]
        sn = jnp.where(live, _dot_nt(q, kn_ref[k].astype(BF16)) + bn_ref[k][:, :NEW_ROWS], NEG_INF)
        m = jnp.maximum(jnp.max(s, axis=-1, keepdims=True), jnp.max(sn, axis=-1, keepdims=True))
        p = jnp.exp(s - m)
        pn = jnp.exp(sn - m)
        den = jnp.sum(p, axis=-1, keepdims=True) + jnp.sum(pn, axis=-1, keepdims=True)
        ow = (jnp.dot(p.astype(BF16), st_ref[:, 1, k, :].astype(BF16), preferred_element_type=F32)
              + jnp.dot(pn.astype(BF16), vn_ref[k].astype(BF16), preferred_element_type=F32)) / den
        o_ref[k] = g_ref[k, 0] * oc_ref[k] + g_ref[k, 1] * os_ref[k] + g_ref[k, 2] * ow


def win_combine_decode(q8, state, layer, k_new, v_new, bias_win, bias_new, o_cmp, o_slc, gates):
    bsz = q8.shape[0]
    wlen = state.shape[2]
    hq = pl.BlockSpec((None, NSA_KV_HEADS, QROWS, HEAD_DIM), lambda b: (b, 0, 0, 0))
    hn = pl.BlockSpec((None, NSA_KV_HEADS, NEW_ROWS, HEAD_DIM), lambda b: (b, 0, 0, 0))
    return pl.pallas_call(
        _win_dec_kernel,
        grid=(bsz,),
        in_specs=[hq,
                  pl.BlockSpec((None, None, wlen, 2, NSA_KV_HEADS, HEAD_DIM), lambda b: (b, layer, 0, 0, 0, 0)),
                  hn, hn,
                  pl.BlockSpec((NSA_KV_HEADS, QROWS, wlen), lambda b: (0, 0, 0)),
                  pl.BlockSpec((NSA_KV_HEADS, QROWS, LANES), lambda b: (0, 0, 0)),
                  hq, hq,
                  pl.BlockSpec((None, NSA_KV_HEADS, 3, QROWS, HEAD_DIM), lambda b: (b, 0, 0, 0, 0))],
        out_specs=hq,
        out_shape=jax.ShapeDtypeStruct((bsz, NSA_KV_HEADS, QROWS, HEAD_DIM), F32),
        compiler_params=_cparams(("parallel",)),
        name="nsa_win_decode",
    )(q8, state, k_new, v_new, bias_win, bias_new, o_cmp, o_slc, gates)


def _decode_tables(rel_bias, past_len, wlen):
    dist = past_len - jnp.arange(past_len + 1, dtype=jnp.int32)
    nf = jnp.maximum(dist, 1).astype(F32)
    large = MAX_EXACT + (jnp.log(nf / MAX_EXACT) / math.log(MAX_DISTANCE / MAX_EXACT)
                         * (N_BUCKETS - MAX_EXACT)).astype(jnp.int32)
    bucket = jnp.where(dist < MAX_EXACT, dist, jnp.minimum(large, N_BUCKETS - 1))
    bias_all = rel_bias[bucket].astype(F32)
    nsa, dif = bias_all[:, :NSA_HEADS], bias_all[:, NSA_HEADS:]
    n_pages, n_blk = past_len // PAGE_SIZE, past_len // CMP_BLOCK
    pad_rows = lambda x, axis: jnp.pad(x, [(0, QROWS - x.shape[axis]) if a == axis else (0, 0) for a in range(x.ndim)])
    lanes = lambda x: jnp.broadcast_to(x[..., None], x.shape + (LANES,))
    diff_t = pad_rows(jnp.transpose(dif[:past_len].reshape(n_pages, PAGE_SIZE, DIFF_HEADS), (0, 2, 1)), 1)
    diff_new = pad_rows(lanes(dif[past_len]), 0)
    nsa_blk = pad_rows(jnp.transpose(nsa[:past_len].reshape(n_blk, CMP_BLOCK, NSA_KV_HEADS, NSA_GROUP), (2, 0, 3, 1)), 2)
    nsa_new = pad_rows(lanes(nsa[past_len].reshape(NSA_KV_HEADS, NSA_GROUP)), 1)
    wdist = dist[past_len - wlen:past_len]
    win = jnp.where((wdist < WINDOW)[None, None, :],
                    jnp.transpose(nsa[past_len - wlen:past_len].reshape(wlen, NSA_KV_HEADS, NSA_GROUP), (1, 2, 0)),
                    NEG_INF)
    return dict(diff_t=diff_t, diff_new=diff_new, nsa_blk=nsa_blk, nsa_new=nsa_new, nsa_win=pad_rows(win, 1))


def _sample_attention(a, s_act, layer, caches, tables, w, nsa_phi_pos, diff_lambda, subln_g, lam_init):
    page_table, cache_fox_kv, lf_t, cache_nsa_kv, cache_diff_kv, state_nsa_win = caches
    bsz = a.shape[0]
    cols = lambda c0, c1: a[:, c0 * LANES:c1 * LANES]
    rep = lambda x: jnp.broadcast_to(x[..., None, :], x.shape[:-1] + (QROWS, x.shape[-1]))
    row0 = lambda x: jnp.pad(x[..., None, :], [(0, 0)] * (x.ndim - 1) + [(0, NEW_ROWS - 1), (0, 0)])
    lanes = lambda x: jnp.broadcast_to(x[..., None], x.shape + (LANES,))

    heads = lambda c0: cols(c0, c0 + FOX_HEADS).reshape(bsz, FOX_HEADS, HEAD_DIM)
    lf_new = lanes(jnp.pad(s_act[:, :N_FF], ((0, 0), (0, LF_ROWS - N_FF))))
    o_fox = fox_decode(page_table, cache_fox_kv, lf_t, layer, rep(heads(COL_FQ)), row0(heads(COL_FK)),
                       row0(heads(COL_FV)), lf_new)[:, :, 0]

    dq = cols(COL_DQ, COL_DK).reshape(bsz, DIFF_HEADS, 2, HEAD_DIM)
    dk = cols(COL_DK, COL_DV).reshape(bsz, DIFF_HEADS, 2, HEAD_DIM)
    dv = cols(COL_DV, COL_DV + 8).reshape(bsz, DIFF_HEADS, DIFF_VDIM)
    o_diff = diff_decode(page_table, cache_diff_kv, tables["diff_t"], layer, rep(dq), row0(dk), row0(dv),
                         tables["diff_new"], diff_lambda, subln_g, lam_init)[:, :, 0]

    nq = cols(COL_NQ, COL_NKV).reshape(bsz, NSA_KV_HEADS, NSA_GROUP, HEAD_DIM)
    q8 = jnp.pad(nq, ((0, 0), (0, 0), (0, QROWS - NSA_GROUP), (0, 0)))
    nkv = cols(COL_NKV, COL_DQ).reshape(bsz, 6, NSA_KV_HEADS, HEAD_DIM)
    pos2 = jnp.tile(nsa_phi_pos, (1, PAGE_SIZE // CMP_BLOCK, 1))
    cmp_kv = compress_decode(page_table, cache_nsa_kv, layer, pos2, w["phi_w1"], w["phi_w2"])
    o_cmp, picked = cmp_select_decode(q8, cmp_kv)
    sel = jnp.pad(picked[:, :, 0, :N_SELECT - 1], ((0, 0), (0, 0), (0, 1))).reshape(bsz, NSA_KV_HEADS * N_SELECT)
    o_slc = slc_decode(page_table, sel, cache_nsa_kv, layer, q8, row0(nkv[:, 2]), row0(nkv[:, 3]),
                       tables["nsa_blk"], tables["nsa_new"])
    ng = s_act[:, N_FF:N_FF + N_NG].reshape(bsz, NSA_KV_HEADS, NSA_GROUP, 3)
    gates = lanes(jnp.pad(jnp.transpose(ng, (0, 1, 3, 2)), ((0, 0), (0, 0), (0, 0), (0, QROWS - NSA_GROUP))))
    o_nsa = win_combine_decode(q8, state_nsa_win, layer, row0(nkv[:, 4]), row0(nkv[:, 5]), tables["nsa_win"],
                               tables["nsa_new"], o_cmp, o_slc, gates)[:, :, :NSA_GROUP]
    win_new = nkv[:, 4:6].reshape(bsz, 1, 2, NSA_KV_HEADS, HEAD_DIM)
    win_state = jnp.concatenate([state_nsa_win[:, layer, 1:], win_new], axis=1)
    return (o_fox.reshape(bsz, FW).astype(BF16), o_nsa.reshape(bsz, FW).astype(BF16),
            o_diff.reshape(bsz, DW).astype(BF16), win_state)


def _prep_layer_weights(l, w_in, fox_fbias, nsa_phi_w1, nsa_phi_w2, w_br_fox, w_br_nsa, w_br_diff, w_o,
                        ffn_w_gate, ffn_w_up, ffn_w_down):
    wl = w_in[l]
    c0, c1 = 3 * FW, 3 * FW + N_FF
    c2 = c1 + FW + NKVW
    c3 = c2 + N_NG
    c4 = c3 + 3 * DW
    w_pack = jnp.concatenate([wl[:, :c0], wl[:, c1:c2], wl[:, c3:c4]], axis=1).astype(BF16)
    w_small = jnp.concatenate([wl[:, c0:c1], wl[:, c2:c3]], axis=1)
    w_small = jnp.pad(w_small, ((0, 0), (0, LANES - w_small.shape[1]))).astype(BF16)
    b_small = jnp.pad(fox_fbias[l], (0, LANES - N_FF)).reshape(1, LANES).astype(F32)
    return dict(
        w_pack=w_pack, w_small=w_small, b_small=b_small, w_bg=wl[:, c4:].astype(BF16),
        phi_w1=nsa_phi_w1[l].astype(BF16), phi_w2=nsa_phi_w2[l].astype(BF16),
        w_br_fox=w_br_fox[l].astype(BF16), w_br_nsa=w_br_nsa[l].astype(BF16),
        w_br_diff=w_br_diff[l].astype(BF16), w_o=w_o[l].astype(BF16),
        layer=l, wg=ffn_w_gate, wu=ffn_w_up, wd=ffn_w_down,
    )


def _ffn(xn, w, i):
    lead = (w["layer"], i)
    act = gateup(xn, w["wg"], w["wu"], lead)
    return matmul(act, w["wd"], tm=512, tn=512, lead=lead, name="ffn_down")


def _layer(x, xn, w, g, g_next, attend):
    h, u = resid_norm(x, _ffn(xn, w, 0), g[1], g[2], 0.5)
    a = matmul(u, w["w_pack"], tm=1024, tn=768, name="proj_pack")
    s_act = small_proj(u, w["w_small"], w["b_small"])
    gates = matmul(u, w["w_bg"], tm=1024, tn=1024, act="sigmoid", name="proj_gates")
    o_fox, o_nsa, o_diff, extra = attend(a, s_act)
    merged = merge_branches(o_fox, o_nsa, o_diff, w["w_br_fox"], w["w_br_nsa"], w["w_br_diff"], gates)
    z = matmul(merged, w["w_o"], tm=1024, tn=1024, name="proj_out")
    h, u3 = resid_norm(h, z, g[3], g[4], 1.0)
    y, yn = resid_norm(h, _ffn(u3, w, 1), g[5], g_next, 0.5)
    return y, yn, a, s_act, extra


def kernel(x_prompt, x_sample, cache_fox_kv, cache_fox_logf, cache_nsa_kv, cache_diff_kv, state_nsa_win,
           page_table, norm_g, w_in, fox_fbias, nsa_phi_w1, nsa_phi_w2, nsa_phi_pos, diff_lambda,
           diff_subln_g, w_br_fox, w_br_nsa, w_br_diff, w_o, ffn_w_gate, ffn_w_up, ffn_w_down, rel_bias):
    bp, t, d = x_prompt.shape
    bs = x_sample.shape[0]
    ms = 16
    hp = x_prompt.reshape(bp * t, d)
    hs = jnp.pad(x_sample.reshape(bs, d), ((0, ms - bs), (0, 0)))
    rb = _rev_bias(rel_bias, t)
    rb_nsa, rb_diff = rb[:NSA_HEADS], rb[NSA_HEADS:]

    tables = _decode_tables(rel_bias, page_table.shape[1] * PAGE_SIZE, state_nsa_win.shape[2])
    lf_t = jnp.pad(jnp.swapaxes(cache_fox_logf, 2, 3), ((0, 0), (0, 0), (0, LF_ROWS - FOX_HEADS), (0, 0)))
    flat = lambda pool: jnp.swapaxes(pool, 3, 4).reshape(pool.shape[:2] + (-1, HEAD_DIM))
    halves = cache_diff_kv.reshape(cache_diff_kv.shape[:5] + (2, HEAD_DIM))
    flat_diff = jnp.swapaxes(halves, 4, 5).reshape(cache_diff_kv.shape[:2] + (-1, HEAD_DIM))
    caches = (page_table, flat(cache_fox_kv), lf_t, flat(cache_nsa_kv), flat_diff, state_nsa_win)

    ffn_w_gate, ffn_w_up, ffn_w_down = (x.astype(BF16) for x in (ffn_w_gate, ffn_w_up, ffn_w_down))

    hpn = rms_cast(hp, norm_g[0, 0])
    hsn = rms_cast(hs, norm_g[0, 0])
    st_p, st_s = [], []
    for l in range(DEPTH):
        w = _prep_layer_weights(l, w_in, fox_fbias, nsa_phi_w1, nsa_phi_w2, w_br_fox, w_br_nsa, w_br_diff,
                                w_o, ffn_w_gate, ffn_w_up, ffn_w_down)
        g = norm_g[l]
        g_next = norm_g[l + 1, 0] if l + 1 < DEPTH else None
        lam_init = 0.8 - 0.6 * math.exp(-0.3 * l)

        def attend_prompt(a, s_act):
            a3 = a.reshape(bp, t, PACK_W)
            s3 = s_act.reshape(bp, t, LANES)
            f_all = cumsum_time(s3)
            f_rows = jnp.transpose(f_all[:, :, :N_FF], (0, 2, 1)).reshape(bp, FOX_HEADS, 1, t)
            o_fox = fox_prompt(a3, f_rows)
            o_diff = diff_prompt(a3, rb_diff, diff_lambda[l], diff_subln_g[l], lam_init)
            cmp_kv = compress_prompt(a3, w["phi_w1"], w["phi_w2"], nsa_phi_pos[l])
            o_cmp, sel = cmp_select_prompt(a3, cmp_kv)
            o_nsa = slc_win_prompt(a3, sel, rb_nsa, o_cmp, s3)
            return (o_fox.reshape(bp * t, FW), o_nsa.reshape(bp * t, FW), o_diff.reshape(bp * t, DW), None)

        def attend_sample(a, s_act):
            o_fox, o_nsa, o_diff, win_state = _sample_attention(
                a[:bs], s_act[:bs], l, caches, tables, w, nsa_phi_pos[l], diff_lambda[l], diff_subln_g[l], lam_init)
            padr = lambda o: jnp.pad(o, ((0, ms - bs), (0, 0)))
            return padr(o_fox), padr(o_nsa), padr(o_diff), win_state

        hp, hpn, a_p, s_p, _ = _layer(hp, hpn, w, g, g_next, attend_prompt)
        hs, hsn, a_s, s_s, win_s = _layer(hs, hsn, w, g, g_next, attend_sample)
        st_p.append((a_p.reshape(bp, t, PACK_W), s_p.reshape(bp, t, LANES)))
        st_s.append((a_s[:bs].reshape(bs, 1, PACK_W), s_s[:bs].reshape(bs, 1, LANES), win_s))

    def states(st, nb_, tt):
        stack = lambda f: jnp.stack([f(s) for s in st], axis=1)
        fox_kv = stack(lambda s: s[0][:, :, FW:3 * FW].reshape(nb_, tt, 2, FOX_HEADS, HEAD_DIM))
        logf = stack(lambda s: s[1][:, :, :N_FF])
        nsa_kv = stack(lambda s: s[0][:, :, COL_NKV * LANES:(COL_NKV + 12) * LANES].reshape(
            nb_, tt, 4, NSA_KV_HEADS, HEAD_DIM))
        diff_kv = stack(lambda s: s[0][:, :, COL_DK * LANES:].reshape(nb_, tt, 2, DIFF_HEADS, DIFF_VDIM))
        return fox_kv, logf, nsa_kv, diff_kv

    fox_kv_p, logf_p, nsa_kv_p, diff_kv_p = states(st_p, bp, t)
    fox_kv_s, logf_s, nsa_kv_s, diff_kv_s = states(st_s, bs, 1)
    win_lo = (COL_NKV + 12) * LANES
    nsa_win_p = jnp.stack([s[0][:, t - min(WINDOW, t):, win_lo:COL_DQ * LANES].reshape(
        bp, min(WINDOW, t), 2, NSA_KV_HEADS, HEAD_DIM) for s in st_p], axis=1)
    nsa_win_s = jnp.stack([s[2] for s in st_s], axis=1)
    return (hp.reshape(bp, t, d), hs[:bs].reshape(bs, 1, d), fox_kv_p, fox_kv_s, logf_p, logf_s,
            nsa_kv_p, nsa_kv_s, diff_kv_p, diff_kv_s, nsa_win_p, nsa_win_s)
```
